```python
import jax, jax.numpy as jnp
from jax import lax
import numpy as np

D_MODEL = 1024
BATCH = 32
SEQ = 2048
DEPTH = 2
DEC_BATCH = 1
DEC_SEQ = 16384
PAST_LEN = 128

NORM_EPS = 1e-6
D_FF = 2816
N_EVEN = (DEPTH + 1) // 2
N_ODD = DEPTH // 2
D_A = D_MODEL // 2
SGU_GROUPS = 4
SGU_CHUNK = 128
SGU_GW = D_A // SGU_GROUPS
D_B = D_MODEL // 2
RWKV_HEAD = 64
RWKV_HEADS = D_B // RWKV_HEAD
RWKV_LORA_W = 64
RWKV_LORA_A = 64
RWKV_LORA = RWKV_LORA_W + RWKV_LORA_A
RWKV_GN_EPS = 64e-5
D_AB_IN = 2 * D_A + 4 * D_B + 2 * RWKV_LORA
ATTN_HEAD_DIM = 64
ATTN_HEADS = D_MODEL // ATTN_HEAD_DIM
D_ATTN = ATTN_HEADS * ATTN_HEAD_DIM
DILATED_BRANCHES = ((128, 1), (512, 4), (2048, 16))
NEG_INF = -1e30

kernel_name = 'hybrid_sgu_rwkv7_dilated_encoder'


def rmsnorm(x, g, eps=NORM_EPS):
    xf = x.astype(jnp.float32)
    y = xf * lax.rsqrt(jnp.mean(xf * xf, axis=-1, keepdims=True) + eps)
    return (y * g.astype(jnp.float32)).astype(x.dtype)


def swiglu_ffn(x, w_in, w_out):
    gate, up = jnp.split(x @ w_in, 2, axis=-1)
    return (jax.nn.silu(gate) * up) @ w_out


def shift_prev(z):
    return jnp.pad(z, ((0, 0), (1, 0), (0, 0)))[:, :-1]


def shift_next(z):
    return jnp.pad(z, ((0, 0), (0, 1), (0, 0)))[:, 1:]


def spatial_gating(u, v, norm_g, w_s, b_s):
    B, T, _ = u.shape
    u = jax.nn.gelu(u)
    v = rmsnorm(jax.nn.gelu(v), norm_g)
    vc = v.reshape(B, T // SGU_CHUNK, SGU_CHUNK, SGU_GROUPS, SGU_GW)
    mixed = jnp.einsum('gqp,bnpgc->bnqgc', w_s, vc) + b_s.T[None, None, :, :, None]
    return u * mixed.reshape(B, T, D_A)


def rwkv7_scan(r, w, k, v, kk, kka, reverse):
    B, T, H, N = r.shape

    def step(S, inp):
        r_t, w_t, k_t, v_t, kk_t, kka_t = inp
        sa = jnp.einsum('bhvk,bhk->bhv', S, kk_t)
        S = (S * w_t[:, :, None, :] - sa[..., None] * kka_t[:, :, None, :]
             + v_t[..., None] * k_t[:, :, None, :])
        return S, jnp.einsum('bhvk,bhk->bhv', S, r_t)

    S0 = jnp.zeros((B, H, N, N), jnp.float32)
    xs = tuple(jnp.swapaxes(t, 0, 1) for t in (r, w, k, v, kk, kka))
    _, y = lax.scan(step, S0, xs, reverse=reverse)
    return jnp.swapaxes(y, 0, 1)


def rwkv7_bidirectional(rkv, wa, g, mu_rkv, mu_wa, w0, w_up, a0, a_up, k_k, k_a, r_k, gn_w, gn_b):
    B, T, _ = rkv.shape
    rkv = rkv.astype(jnp.float32)
    wa = wa.astype(jnp.float32)
    heads = lambda t: t.reshape(B, T, RWKV_HEADS, RWKV_HEAD)
    ys, bonuses = [], []
    for dirn, (shift, rev) in enumerate(((shift_prev, False), (shift_next, True))):
        z = rkv + (shift(rkv) - rkv) * mu_rkv[dirn]
        zwa = wa[:, :, dirn]
        zwa = zwa + (shift(zwa) - zwa) * mu_wa[dirn]
        r, k, v = jnp.split(z, 3, axis=-1)
        wd, ad = jnp.split(zwa, [RWKV_LORA_W], axis=-1)
        w_raw = w0[dirn] + jnp.tanh(wd) @ w_up[dirn]
        decay = jnp.exp(-jnp.exp(-jax.nn.softplus(-w_raw) - 0.5))
        a = jax.nn.sigmoid(a0[dirn] + ad @ a_up[dirn])
        kk = heads(k * k_k)
        kk = kk / jnp.maximum(jnp.sqrt(jnp.sum(kk * kk, axis=-1, keepdims=True)), 1e-12)
        k = k * (1.0 + (a - 1.0) * k_a)
        rh, kh, vh = heads(r), heads(k), heads(v)
        ys.append(rwkv7_scan(rh, heads(decay), kh, vh, kk, kk * heads(a), rev))
        bonuses.append(jnp.sum(rh * kh * r_k, axis=-1, keepdims=True) * vh)
    y = ys[0] + ys[1]
    mean = jnp.mean(y, axis=-1, keepdims=True)
    var = jnp.mean(jnp.square(y - mean), axis=-1, keepdims=True)
    y = ((y - mean) * lax.rsqrt(var + RWKV_GN_EPS)).reshape(B, T, D_B) * gn_w + gn_b
    y = (y + (bonuses[0] + bonuses[1]).reshape(B, T, D_B)) * jax.nn.sigmoid(g.astype(jnp.float32))
    return y


def sgu_rwkv_mixer(h, w_in, w_out, sgu_norm, sgu_w, sgu_b, mu_rkv, mu_wa, w0, w_up, a0, a_up,
                   k_k, k_a, r_k, gn_w, gn_b):
    B, T, _ = h.shape
    p = h @ w_in
    u, v, rkv, g, wa = jnp.split(p, [D_A, 2 * D_A, 2 * D_A + 3 * D_B, 2 * D_A + 4 * D_B], axis=-1)
    y_a = spatial_gating(u, v, sgu_norm, sgu_w, sgu_b)
    y_b = rwkv7_bidirectional(rkv, wa.reshape(B, T, 2, RWKV_LORA), g, mu_rkv, mu_wa, w0, w_up,
                              a0, a_up, k_k, k_a, r_k, gn_w, gn_b)
    return jnp.concatenate([y_a, y_b.astype(y_a.dtype)], axis=-1) @ w_out


def alibi_slopes():
    return 2.0 ** (-8.0 * jnp.arange(1, ATTN_HEADS + 1, dtype=jnp.float32) / ATTN_HEADS)


def dilated_branch(q, k, v, slopes, window, dilation):
    B, T, H, E = q.shape
    rad = window // (2 * dilation)
    L = T // dilation
    nb = -(-L // rad)
    Lp = nb * rad

    def to_sub(t):
        return jnp.swapaxes(t.reshape(B, L, dilation, H, E), 1, 2)

    def windows(t):
        tp = jnp.pad(t, ((0, 0), (0, 0), (rad, Lp - L + rad), (0, 0), (0, 0)))
        tp = tp.reshape(B, dilation, nb + 2, rad, H, E)
        return jnp.concatenate([tp[:, :, 0:nb], tp[:, :, 1:nb + 1], tp[:, :, 2:nb + 2]], axis=3)

    qb = jnp.pad(to_sub(q), ((0, 0), (0, 0), (0, Lp - L), (0, 0), (0, 0))).reshape(B, dilation, nb, rad, H, E)
    kw, vw = windows(to_sub(k)), windows(to_sub(v))
    qi = jnp.arange(Lp).reshape(nb, rad)
    kj = jnp.arange(nb)[:, None] * rad - rad + jnp.arange(3 * rad)[None, :]
    dist = jnp.abs(qi[:, :, None] - kj[:, None, :])
    valid = (dist <= rad) & (kj[:, None, :] >= 0) & (kj[:, None, :] < L)
    bias = jnp.where(valid, -slopes[:, None, None, None] * (dist * dilation).astype(jnp.float32),
                     NEG_INF)
    s = jnp.einsum('bdnqhe,bdnkhe->bdhnqk', qb, kw) + bias
    m = jnp.max(s, axis=-1, keepdims=True)
    p = jnp.exp(s - m)
    den = jnp.sum(p, axis=-1)
    o = jnp.einsum('bdhnqk,bdnkhe->bdnqhe', p, vw) / jnp.moveaxis(den, 2, -1)[..., None]
    lse = jnp.moveaxis(m[..., 0] + jnp.log(den), 2, -1)
    o = jnp.swapaxes(o.reshape(B, dilation, Lp, H, E)[:, :, :L], 1, 2).reshape(B, T, H, E)
    lse = jnp.swapaxes(lse.reshape(B, dilation, Lp, H)[:, :, :L], 1, 2).reshape(B, T, H)
    return o, lse


def dilated_attention(h, w_in, w_out, q_norm, k_norm):
    B, T, _ = h.shape
    qkv = (h @ w_in).astype(jnp.float32).reshape(B, T, 3, ATTN_HEADS, ATTN_HEAD_DIM)
    q = rmsnorm(qkv[:, :, 0], q_norm) * (ATTN_HEAD_DIM ** -0.5)
    k = rmsnorm(qkv[:, :, 1], k_norm)
    v = qkv[:, :, 2]
    slopes = alibi_slopes()
    outs, lses = [], []
    for window, dilation in DILATED_BRANCHES:
        o_b, lse_b = dilated_branch(q, k, v, slopes, window, dilation)
        outs.append(o_b)
        lses.append(lse_b)
    wts = jax.nn.softmax(jnp.stack(lses), axis=0)
    o = jnp.sum(wts[..., None] * jnp.stack(outs), axis=0)
    return o.reshape(B, T, D_ATTN).astype(h.dtype) @ w_out


def setup_inputs(seed: int = 0) -> dict:
    key = jax.random.key(seed)
    ks = iter(jax.random.split(key, 40))
    nrm = lambda shape, scale: jax.random.normal(next(ks), shape, jnp.float32) * scale
    unif = lambda shape, lo, hi: jax.random.uniform(next(ks), shape, jnp.float32, lo, hi)
    D, F = D_MODEL, D_FF
    return {
        'x_prompt': nrm((BATCH, SEQ, D), 1.0),
        'x_sample': nrm((DEC_BATCH, DEC_SEQ, D), 1.0),
        'ffn1_norm': 1.0 + nrm((DEPTH, D), 0.02),
        'ffn1_w_in': nrm((DEPTH, D, 2 * F), D ** -0.5),
        'ffn1_w_out': nrm((DEPTH, F, D), F ** -0.5),
        'mix_norm': 1.0 + nrm((DEPTH, D), 0.02),
        'ffn2_norm': 1.0 + nrm((DEPTH, D), 0.02),
        'ffn2_w_in': nrm((DEPTH, D, 2 * F), D ** -0.5),
        'ffn2_w_out': nrm((DEPTH, F, D), F ** -0.5),
        'block_norm': 1.0 + nrm((DEPTH, D), 0.02),
        'ab_w_in': nrm((N_EVEN, D, D_AB_IN), D ** -0.5),
        'ab_w_out': nrm((N_EVEN, D_A + D_B, D), (D_A + D_B) ** -0.5),
        'sgu_norm': 1.0 + nrm((N_EVEN, D_A), 0.02),
        'sgu_w': nrm((N_EVEN, SGU_GROUPS, SGU_CHUNK, SGU_CHUNK), SGU_CHUNK ** -0.5),
        'sgu_b': 1.0 + nrm((N_EVEN, SGU_GROUPS, SGU_CHUNK), 0.02),
        'rwkv_mu_rkv': unif((N_EVEN, 2, 3 * D_B), 0.0, 1.0),
        'rwkv_mu_wa': unif((N_EVEN, 2, RWKV_LORA), 0.0, 1.0),
        'rwkv_w0': unif((N_EVEN, 2, D_B), -2.0, 1.0),
        'rwkv_w_up': nrm((N_EVEN, 2, RWKV_LORA_W, D_B), 0.1 * RWKV_LORA_W ** -0.5),
        'rwkv_a0': nrm((N_EVEN, 2, D_B), 0.5),
        'rwkv_a_up': nrm((N_EVEN, 2, RWKV_LORA_A, D_B), 0.1 * RWKV_LORA_A ** -0.5),
        'rwkv_k_k': 0.85 + nrm((N_EVEN, D_B), 0.02),
        'rwkv_k_a': 1.0 + nrm((N_EVEN, D_B), 0.02),
        'rwkv_r_k': nrm((N_EVEN, RWKV_HEADS, RWKV_HEAD), 0.1),
        'rwkv_gn_w': 1.0 + nrm((N_EVEN, D_B), 0.02),
        'rwkv_gn_b': nrm((N_EVEN, D_B), 0.02),
        'attn_w_in': nrm((N_ODD, D, 3 * D_ATTN), D ** -0.5),
        'attn_w_out': nrm((N_ODD, D_ATTN, D), D_ATTN ** -0.5),
        'attn_q_norm': 1.0 + nrm((N_ODD, ATTN_HEAD_DIM), 0.02),
        'attn_k_norm': 1.0 + nrm((N_ODD, ATTN_HEAD_DIM), 0.02),
    }


def reference(x_prompt, x_sample, ffn1_norm, ffn1_w_in, ffn1_w_out, mix_norm, ffn2_norm, ffn2_w_in,
              ffn2_w_out, block_norm, ab_w_in, ab_w_out, sgu_norm, sgu_w, sgu_b, rwkv_mu_rkv, rwkv_mu_wa,
              rwkv_w0, rwkv_w_up, rwkv_a0, rwkv_a_up, rwkv_k_k, rwkv_k_a, rwkv_r_k, rwkv_gn_w, rwkv_gn_b,
              attn_w_in, attn_w_out, attn_q_norm, attn_k_norm):
    def trunk(x):
        for i in range(DEPTH):
            j = i // 2
            x = x + 0.5 * swiglu_ffn(rmsnorm(x, ffn1_norm[i]), ffn1_w_in[i], ffn1_w_out[i])
            h = rmsnorm(x, mix_norm[i])
            if i % 2 == 0:
                x = x + sgu_rwkv_mixer(h, ab_w_in[j], ab_w_out[j], sgu_norm[j], sgu_w[j], sgu_b[j],
                                       rwkv_mu_rkv[j], rwkv_mu_wa[j], rwkv_w0[j], rwkv_w_up[j],
                                       rwkv_a0[j], rwkv_a_up[j], rwkv_k_k[j], rwkv_k_a[j], rwkv_r_k[j],
                                       rwkv_gn_w[j], rwkv_gn_b[j])
            else:
                x = x + dilated_attention(h, attn_w_in[j], attn_w_out[j], attn_q_norm[j], attn_k_norm[j])
            x = x + 0.5 * swiglu_ffn(rmsnorm(x, ffn2_norm[i]), ffn2_w_in[i], ffn2_w_out[i])
            x = rmsnorm(x, block_norm[i])
        return x

    y_prompt = trunk(x_prompt)
    y_sample = trunk(x_sample)
    return (y_prompt, y_sample)
```

```python
import functools
import math

import numpy as np
import jax
import jax.numpy as jnp
from jax import lax
from jax.experimental import pallas as pl
from jax.experimental.pallas import tpu as pltpu

F32 = jnp.float32
BF16 = jnp.bfloat16

NORM_EPS = 1e-6
SGU_GROUPS = 4
SGU_CHUNK = 128
RWKV_HEAD = 64
RWKV_LORA_W = 64
RWKV_GN_EPS = 64e-5
ATTN_HEAD_DIM = 64
ATTN_RADIUS = 64
DILATIONS = (1, 4, 16)
NEG_INF = -1e30

LANES = 128
SCAN_CHUNK = 64
VMEM_LIMIT_BYTES = 52 * 1024 * 1024


def _params(*semantics):
    return pltpu.CompilerParams(dimension_semantics=semantics, vmem_limit_bytes=VMEM_LIMIT_BYTES)


def _dot(a, b):
    return jnp.dot(a.astype(BF16), b.astype(BF16), preferred_element_type=F32)


def _dot_nt(a, b):
    return lax.dot_general(a.astype(BF16), b.astype(BF16), (((1,), (1,)), ((), ())),
                           preferred_element_type=F32)


def _dot_tn(a, b):
    return lax.dot_general(a.astype(BF16), b.astype(BF16), (((0,), (0,)), ((), ())),
                           preferred_element_type=F32)


def _split(x):
    hi = x.astype(BF16)
    lo = (x - hi.astype(F32)).astype(BF16)
    return hi, lo


def _dot_acc(dot, a, b, passes):
    if passes == 1:
        return dot(a, b)
    ah, al = _split(a)
    bh, bl = _split(b)
    return dot(ah, bh) + (dot(ah, bl) + dot(al, bh))


def _dot_exact_lhs(lhs_bf16, x):
    hi, lo = _split(x)
    lo2 = (x - hi.astype(F32) - lo.astype(F32)).astype(BF16)
    return _dot(lhs_bf16, hi) + (_dot(lhs_bf16, lo) + _dot(lhs_bf16, lo2))


def _dot_exact_rhs(x, rhs_bf16):
    hi, lo = _split(x)
    return _dot(hi, rhs_bf16) + _dot(lo, rhs_bf16)


def _rms(x, g, eps=NORM_EPS):
    return x * lax.rsqrt(jnp.mean(x * x, axis=-1, keepdims=True) + eps) * g


def _row_tile(n, want):
    t = min(n, want)
    assert n % t == 0, (n, t)
    return t


def _ffn_body(x_ref, g_ref, wg_ref, wu_ref, wo_ref, g2_ref, o_ref, h_ref, acc_ref, *, final_norm):
    f = pl.program_id(1)

    @pl.when(f == 0)
    def _():
        h_ref[...] = _rms(x_ref[...], g_ref[...]).astype(BF16)
        acc_ref[...] = jnp.zeros_like(acc_ref)

    h = h_ref[...]
    gate = jnp.dot(h, wg_ref[...], preferred_element_type=F32)
    up = jnp.dot(h, wu_ref[...], preferred_element_type=F32)
    act = (gate * jax.nn.sigmoid(gate) * up).astype(BF16)
    acc_ref[...] += jnp.dot(act, wo_ref[...], preferred_element_type=F32)

    @pl.when(f == pl.num_programs(1) - 1)
    def _():
        y = x_ref[...] + 0.5 * acc_ref[...]
        if final_norm:
            y = _rms(y, g2_ref[...])
        o_ref[...] = y


def _ffn(x, g, w_in, w_out, g2, *, final_norm):
    n, d = x.shape
    f_dim = w_out.shape[0]
    tm = _row_tile(n, 512)
    nf = 2
    tf = f_dim // nf
    assert tf * nf == f_dim and tf % LANES == 0
    w_in = w_in.astype(BF16)
    w_out = w_out.astype(BF16)
    return pl.pallas_call(
        functools.partial(_ffn_body, final_norm=final_norm),
        grid=(n // tm, nf),
        in_specs=[
            pl.BlockSpec((tm, d), lambda i, f: (i, 0)),
            pl.BlockSpec((1, d), lambda i, f: (0, 0)),
            pl.BlockSpec((d, tf), lambda i, f: (0, f)),
            pl.BlockSpec((d, tf), lambda i, f: (0, f + nf)),
            pl.BlockSpec((tf, d), lambda i, f: (f, 0)),
            pl.BlockSpec((1, d), lambda i, f: (0, 0)),
        ],
        out_specs=pl.BlockSpec((tm, d), lambda i, f: (i, 0)),
        out_shape=jax.ShapeDtypeStruct((n, d), F32),
        scratch_shapes=[pltpu.VMEM((tm, d), BF16), pltpu.VMEM((tm, d), F32)],
        compiler_params=_params("parallel", "arbitrary"),
    )(x, g.reshape(1, d), w_in, w_in, w_out, g2.reshape(1, d))


def _proj_sgu_body(x_ref, g_ref, w_ref, sgn_ref, sw_ref, sb_ref, ya_ref, rkv_ref, gate_ref, wa_ref,
                   *, d_a, d_b):
    h = _rms(x_ref[...], g_ref[...]).astype(BF16)
    o_rkv, o_g, o_wa = 2 * d_a, 2 * d_a + 3 * d_b, 2 * d_a + 4 * d_b
    u = jnp.dot(h, w_ref[:, 0:d_a], preferred_element_type=F32)
    v = jnp.dot(h, w_ref[:, d_a:2 * d_a], preferred_element_type=F32)
    rkv_ref[...] = jnp.dot(h, w_ref[:, o_rkv:o_g], preferred_element_type=F32)
    gate_ref[...] = jnp.dot(h, w_ref[:, o_g:o_wa], preferred_element_type=F32)
    wa_ref[...] = jnp.dot(h, w_ref[:, o_wa:], preferred_element_type=F32)
    u = jax.nn.gelu(u)
    v = _rms(jax.nn.gelu(v), sgn_ref[...]).astype(BF16)
    gw = d_a // SGU_GROUPS
    for c in range(x_ref.shape[0] // SGU_CHUNK):
        rows = slice(c * SGU_CHUNK, (c + 1) * SGU_CHUNK)
        for grp in range(SGU_GROUPS):
            cols = slice(grp * gw, (grp + 1) * gw)
            mixed = jnp.dot(sw_ref[grp], v[rows, cols], preferred_element_type=F32) + sb_ref[:, cols]
            ya_ref[rows, cols] = (u[rows, cols] * mixed).astype(BF16)


def _proj_sgu(x, g, w_in, sgu_norm, sgu_w, sgu_b, *, d_a, d_b):
    n, d = x.shape
    d_in = w_in.shape[1]
    d_wa = d_in - 2 * d_a - 4 * d_b
    tm = _row_tile(n, 512)
    gw = d_a // SGU_GROUPS
    bias = jnp.repeat(sgu_b.T, gw, axis=1)
    return pl.pallas_call(
        functools.partial(_proj_sgu_body, d_a=d_a, d_b=d_b),
        grid=(n // tm,),
        in_specs=[
            pl.BlockSpec((tm, d), lambda i: (i, 0)),
            pl.BlockSpec((1, d), lambda i: (0, 0)),
            pl.BlockSpec((d, d_in), lambda i: (0, 0)),
            pl.BlockSpec((1, d_a), lambda i: (0, 0)),
            pl.BlockSpec((SGU_GROUPS, SGU_CHUNK, SGU_CHUNK), lambda i: (0, 0, 0)),
            pl.BlockSpec((SGU_CHUNK, d_a), lambda i: (0, 0)),
        ],
        out_specs=[
            pl.BlockSpec((tm, d_a), lambda i: (i, 0)),
            pl.BlockSpec((tm, 3 * d_b), lambda i: (i, 0)),
            pl.BlockSpec((tm, d_b), lambda i: (i, 0)),
            pl.BlockSpec((tm, d_wa), lambda i: (i, 0)),
        ],
        out_shape=[
            jax.ShapeDtypeStruct((n, d_a), BF16),
            jax.ShapeDtypeStruct((n, 3 * d_b), F32),
            jax.ShapeDtypeStruct((n, d_b), F32),
            jax.ShapeDtypeStruct((n, d_wa), F32),
        ],
        compiler_params=_params("parallel"),
    )(x, g.reshape(1, d), w_in.astype(BF16), sgu_norm.reshape(1, d_a), sgu_w.astype(BF16), bias)


PASSES_GRAM = 3
PASSES_INV = 3
PASSES_STATE = 3


def _rwkv_scan_body(rkvf_ref, hrf_ref, waf_ref, hwf_ref, rkvb_ref, hrb_ref, wab_ref, hwb_ref,
                    tri_ref, gsum_ref, mu_rkv_ref, mu_wa_ref, w0_ref, wup_ref, a0_ref, aup_ref,
                    kk_ref, ka_ref, rk_ref,
                    yf_ref, bonf_ref, yb_ref, bonb_ref,
                    s_ref, a_s, k_s, b_s, r_s, v_s, cum_s, *, tb, d_b):
    n = pl.program_id(1)
    nc = tb // SCAN_CHUNK
    heads = d_b // RWKV_HEAD
    first = n == 0
    row = lax.broadcasted_iota(jnp.int32, (tb, 1), 0)
    gsum = gsum_ref[...]

    @pl.when(first)
    def _():
        s_ref[...] = jnp.zeros_like(s_ref)

    def group_sum(x):
        return _dot_exact_rhs(x, gsum)

    def prep(d, x, halo, wa, wa_halo, bon_ref):
        halo = jnp.where(first, 0.0, halo)
        wa_halo = jnp.where(first, 0.0, wa_halo)
        if d == 0:
            xs = jnp.where(row == 0, halo, pltpu.roll(x, 1, 0))
            was = jnp.where(row == 0, wa_halo, pltpu.roll(wa, 1, 0))
        else:
            xs = jnp.where(row == tb - 1, halo, pltpu.roll(x, tb - 1, 0))
            was = jnp.where(row == tb - 1, wa_halo, pltpu.roll(wa, tb - 1, 0))
        z = x + (xs - x) * mu_rkv_ref[d:d + 1, :]
        zwa = wa + (was - wa) * mu_wa_ref[d:d + 1, :]
        r, k, v = z[:, 0:d_b], z[:, d_b:2 * d_b], z[:, 2 * d_b:3 * d_b]
        w_raw = w0_ref[d:d + 1, :] + _dot(jnp.tanh(zwa), wup_ref[d])
        lw = -jax.nn.sigmoid(w_raw) * math.exp(-0.5)
        a = jax.nn.sigmoid(a0_ref[d:d + 1, :] + _dot(zwa, aup_ref[d]))
        kk = k * kk_ref[...]
        kk = kk / jnp.maximum(jnp.sqrt(group_sum(kk * kk)), 1e-12)
        k2 = k * (1.0 + (a - 1.0) * ka_ref[...])
        bon_ref[...] = group_sum(r * k2 * rk_ref[...]) * v
        cum = _dot_exact_lhs(tri_ref[d], lw)
        e_neg = jnp.exp(-cum)
        a_s[d] = kk * a * e_neg
        k_s[d] = k2 * e_neg
        b_s[d] = kk * jnp.exp(cum - lw)
        r_s[d] = r * jnp.exp(cum)
        v_s[d] = v
        cum_s[d] = cum

    prep(0, rkvf_ref[...], hrf_ref[7:8, :], waf_ref[...], hwf_ref[7:8, :], bonf_ref)
    prep(1, rkvb_ref[...], hrb_ref[0:1, :], wab_ref[...], hwb_ref[0:1, :], bonb_ref)

    ri = lax.broadcasted_iota(jnp.int32, (SCAN_CHUNK, SCAN_CHUNK), 0)
    ci = lax.broadcasted_iota(jnp.int32, (SCAN_CHUNK, SCAN_CHUNK), 1)
    eye = (ri == ci).astype(F32)
    y_refs = (yf_ref, yb_ref)

    def chunk_step(c, carry):
        for d in range(2):
            cc = c if d == 0 else nc - 1 - c
            r0 = pl.multiple_of(cc * SCAN_CHUNK, SCAN_CHUNK)
            rows = pl.ds(r0, SCAN_CHUNK)
            strict = (ci < ri) if d == 0 else (ci > ri)
            incl = (ci <= ri) if d == 0 else (ci >= ri)
            last = r0 + (SCAN_CHUNK - 1 if d == 0 else 0)
            p_tot = jnp.exp(cum_s[d, pl.ds(last, 1), :])
            for h in range(heads):
                cols = slice(h * RWKV_HEAD, (h + 1) * RWKV_HEAD)
                am, km = a_s[d, rows, cols], k_s[d, rows, cols]
                bm, rm = b_s[d, rows, cols], r_s[d, rows, cols]
                vm = v_s[d, rows, cols]
                br = jnp.concatenate([bm, rm], axis=0)
                ga = _dot_acc(_dot_nt, br, am, PASSES_GRAM)
                gk = _dot_acc(_dot_nt, br, km, PASSES_GRAM)
                l_ab = jnp.where(strict, ga[:SCAN_CHUNK], 0.0)
                l_bk = jnp.where(strict, gk[:SCAN_CHUNK], 0.0)
                m_ra = jnp.where(incl, ga[SCAN_CHUNK:], 0.0)
                m_rk = jnp.where(incl, gk[SCAN_CHUNK:], 0.0)
                pw = -l_ab
                t_inv = eye + pw
                for _ in range(5):
                    pw = _dot_acc(_dot, pw, pw, PASSES_INV)
                    t_inv = t_inv + _dot_acc(_dot, t_inv, pw, PASSES_INV)
                s0 = s_ref[d, h]
                rhs = _dot_acc(_dot_nt, bm, s0, PASSES_STATE) + _dot_acc(_dot, l_bk, vm, PASSES_STATE)
                u = -_dot_acc(_dot, t_inv, rhs, PASSES_STATE)
                y = (_dot_acc(_dot_nt, rm, s0, PASSES_STATE) + _dot_acc(_dot, m_ra, u, PASSES_STATE)
                     + _dot_acc(_dot, m_rk, vm, PASSES_STATE))
                s_new = s0 + _dot_acc(_dot_tn, u, am, PASSES_STATE) + _dot_acc(_dot_tn, vm, km, PASSES_STATE)
                s_ref[d, h] = s_new * p_tot[:, cols]
                y_refs[d][rows, cols] = y
        return carry

    lax.fori_loop(0, nc, chunk_step, 0)


def _rwkv_scan(rkv, wa, batch, seq, mu_rkv, mu_wa, w0, w_up, a0, a_up, k_k, k_a, r_k):
    n, d3 = rkv.shape
    d_b = d3 // 3
    lora = wa.shape[1] // 2
    tb = _row_tile(seq, 256)
    nb = seq // tb
    hb = tb // 8
    last8 = n // 8 - 1

    pos = np.arange(tb)
    same = (pos[:, None] // SCAN_CHUNK) == (pos[None, :] // SCAN_CHUNK)
    tri = np.stack([same & (pos[None, :] <= pos[:, None]), same & (pos[None, :] >= pos[:, None])])
    tri = jnp.asarray(tri, BF16)
    ch = np.arange(d_b) // RWKV_HEAD
    gsum = jnp.asarray(ch[:, None] == ch[None, :], BF16)
    zpad = jnp.zeros((2, lora - RWKV_LORA_W, d_b), F32)
    wup = jnp.concatenate([w_up, jnp.zeros((2, lora - w_up.shape[1], d_b), F32)], axis=1).astype(BF16)
    aup = jnp.concatenate([jnp.zeros((2, lora - a_up.shape[1], d_b), F32), a_up], axis=1).astype(BF16)
    del zpad

    fwd = lambda b, i: (b * nb + i, 0)
    bwd = lambda b, i: (b * nb + nb - 1 - i, 0)
    fwd_halo = lambda b, i: (jnp.maximum((b * nb + i) * hb - 1, 0), 0)
    bwd_halo = lambda b, i: (jnp.minimum((b * nb + nb - i) * hb, last8), 0)
    const2 = lambda b, i: (0, 0)
    const3 = lambda b, i: (0, 0, 0)
    out_spec_f = pl.BlockSpec((tb, d_b), fwd)
    out_spec_b = pl.BlockSpec((tb, d_b), bwd)
    out_sds = jax.ShapeDtypeStruct((n, d_b), F32)
    return pl.pallas_call(
        functools.partial(_rwkv_scan_body, tb=tb, d_b=d_b),
        grid=(batch, nb),
        in_specs=[
            pl.BlockSpec((tb, d3), fwd), pl.BlockSpec((8, d3), fwd_halo),
            pl.BlockSpec((tb, lora), fwd), pl.BlockSpec((8, lora), fwd_halo),
            pl.BlockSpec((tb, d3), bwd), pl.BlockSpec((8, d3), bwd_halo),
            pl.BlockSpec((tb, lora), lambda b, i: (b * nb + nb - 1 - i, 1)),
            pl.BlockSpec((8, lora), lambda b, i: (jnp.minimum((b * nb + nb - i) * hb, last8), 1)),
            pl.BlockSpec((2, tb, tb), const3),
            pl.BlockSpec((d_b, d_b), const2),
            pl.BlockSpec((2, d3), const2),
            pl.BlockSpec((2, lora), const2),
            pl.BlockSpec((2, d_b), const2),
            pl.BlockSpec((2, lora, d_b), const3),
            pl.BlockSpec((2, d_b), const2),
            pl.BlockSpec((2, lora, d_b), const3),
            pl.BlockSpec((1, d_b), const2),
            pl.BlockSpec((1, d_b), const2),
            pl.BlockSpec((1, d_b), const2),
        ],
        out_specs=[out_spec_f, out_spec_f, out_spec_b, out_spec_b],
        out_shape=[out_sds] * 4,
        scratch_shapes=[pltpu.VMEM((2, d_b // RWKV_HEAD, RWKV_HEAD, RWKV_HEAD), F32)]
        + [pltpu.VMEM((2, tb, d_b), F32)] * 6,
        compiler_params=_params("parallel", "arbitrary"),
    )(rkv, rkv, wa, wa, rkv, rkv, wa, wa, tri, gsum, mu_rkv, mu_wa, w0, wup, a0, aup,
      k_k.reshape(1, d_b), k_a.reshape(1, d_b), r_k.reshape(1, d_b))


def _rwkv_out_body(x_ref, ya_ref, yf_ref, yb_ref, bonf_ref, bonb_ref, gate_ref, gmean_ref,
                   gnw_ref, gnb_ref, woa_ref, wob_ref, o_ref):
    gmean = gmean_ref[...]
    y = yf_ref[...] + yb_ref[...]
    dev = y - _dot_exact_rhs(y, gmean)
    var = _dot_exact_rhs(dev * dev, gmean)
    y = dev * lax.rsqrt(var + RWKV_GN_EPS) * gnw_ref[...] + gnb_ref[...]
    y = (y + (bonf_ref[...] + bonb_ref[...])) * jax.nn.sigmoid(gate_ref[...])
    o_ref[...] = (x_ref[...] + jnp.dot(ya_ref[...], woa_ref[...], preferred_element_type=F32)
                  + jnp.dot(y.astype(BF16), wob_ref[...], preferred_element_type=F32))


def _rwkv_out(x, ya, yf, yb, bonf, bonb, gate, gn_w, gn_b, w_out):
    n, d = x.shape
    d_a, d_b = ya.shape[1], yf.shape[1]
    tm = _row_tile(n, 512)
    ch = np.arange(d_b) // RWKV_HEAD
    gmean = jnp.asarray((ch[:, None] == ch[None, :]) / RWKV_HEAD, BF16)
    w_out = w_out.astype(BF16)
    tok = lambda w: pl.BlockSpec((tm, w), lambda i: (i, 0))
    const = lambda r, c: pl.BlockSpec((r, c), lambda i: (0, 0))
    return pl.pallas_call(
        _rwkv_out_body,
        grid=(n // tm,),
        in_specs=[tok(d), tok(d_a), tok(d_b), tok(d_b), tok(d_b), tok(d_b), tok(d_b),
                  const(d_b, d_b), const(1, d_b), const(1, d_b), const(d_a, d), const(d_b, d)],
        out_specs=tok(d),
        out_shape=jax.ShapeDtypeStruct((n, d), F32),
        compiler_params=_params("parallel"),
    )(x, ya, yf, yb, bonf, bonb, gate, gmean, gn_w.reshape(1, d_b), gn_b.reshape(1, d_b),
      w_out[:d_a], w_out[d_a:])


def _attn_proj_body(x_ref, g_ref, w_ref, hmean_ref, qn_ref, kn_ref, q_ref, k_ref, v_ref, *, d_attn):
    h = _rms(x_ref[...], g_ref[...]).astype(BF16)
    hmean = hmean_ref[...]

    def head_rms(t, gain):
        ms = jnp.concatenate(
            [_dot_exact_rhs(t[:, c:c + LANES] * t[:, c:c + LANES], hmean) for c in range(0, d_attn, LANES)],
            axis=1)
        return t * lax.rsqrt(ms + NORM_EPS) * gain

    q = jnp.dot(h, w_ref[:, 0:d_attn], preferred_element_type=F32)
    q_ref[...] = (head_rms(q, qn_ref[...]) * (ATTN_HEAD_DIM ** -0.5)).astype(BF16)
    k = jnp.dot(h, w_ref[:, d_attn:2 * d_attn], preferred_element_type=F32)
    k_ref[...] = head_rms(k, kn_ref[...]).astype(BF16)
    v_ref[...] = jnp.dot(h, w_ref[:, 2 * d_attn:], preferred_element_type=F32).astype(BF16)


def _attn_proj(x, g, w_in, q_norm, k_norm):
    n, d = x.shape
    d_attn = w_in.shape[1] // 3
    heads = d_attn // ATTN_HEAD_DIM
    tm = _row_tile(n, 512)
    ch = np.arange(LANES) // ATTN_HEAD_DIM
    hmean = jnp.asarray((ch[:, None] == ch[None, :]) / ATTN_HEAD_DIM, BF16)
    tok = pl.BlockSpec((tm, d_attn), lambda i: (i, 0))
    const = lambda r, c: pl.BlockSpec((r, c), lambda i: (0, 0))
    sds = jax.ShapeDtypeStruct((n, d_attn), BF16)
    return pl.pallas_call(
        functools.partial(_attn_proj_body, d_attn=d_attn),
        grid=(n // tm,),
        in_specs=[pl.BlockSpec((tm, d), lambda i: (i, 0)), const(1, d), const(d, 3 * d_attn),
                  const(LANES, LANES), const(1, d_attn), const(1, d_attn)],
        out_specs=[tok, tok, tok],
        out_shape=[sds, sds, sds],
        compiler_params=_params("parallel"),
    )(x, g.reshape(1, d), w_in.astype(BF16), hmean,
      jnp.tile(q_norm, heads).reshape(1, d_attn), jnp.tile(k_norm, heads).reshape(1, d_attn))


def _band_attn_body(slopes_ref, q_ref, kc_ref, kp_ref, kn_ref, vc_ref, vp_ref, vn_ref, o_ref, lse_ref,
                    *, tq, nq, seq, dilation, d_attn):
    t0 = pl.program_id(1) * tq
    nk = nq + 2 * ATTN_RADIUS
    lane = lax.broadcasted_iota(jnp.int32, (1, LANES), 1)
    qi = lax.broadcasted_iota(jnp.int32, (nq, nk), 0)
    kj = lax.broadcasted_iota(jnp.int32, (nq, nk), 1) - ATTN_RADIUS
    dist = jnp.abs(qi - kj)
    dist_f = (dist * dilation).astype(F32)
    in_band = dist <= ATTN_RADIUS
    for hp in range(d_attn // LANES):
        cols = slice(hp * LANES, (hp + 1) * LANES)
        kfull = jnp.concatenate([kp_ref[:, cols], kc_ref[:, cols], kn_ref[:, cols]], axis=0)
        vfull = jnp.concatenate([vp_ref[:, cols], vc_ref[:, cols], vn_ref[:, cols]], axis=0)
        for sb in range(tq // nq):
            rows = slice(sb * nq, (sb + 1) * nq)
            qs = q_ref[rows, cols]
            kw = kfull[sb * nq:sb * nq + nk]
            vw = vfull[sb * nq:sb * nq + nk]
            kpos = t0 + sb * nq + kj
            valid = in_band & (kpos >= 0) & (kpos < seq)
            outs, lses = [], []
            for hh in range(LANES // ATTN_HEAD_DIM):
                in_head = (lane // ATTN_HEAD_DIM) == hh
                slope = slopes_ref[hp * (LANES // ATTN_HEAD_DIM) + hh]
                s = _dot_nt(jnp.where(in_head, qs, jnp.zeros_like(qs)), kw)
                s = jnp.where(valid, s - slope * dist_f, NEG_INF)
                m = jnp.max(s, axis=-1, keepdims=True)
                p = jnp.exp(s - m)
                den = jnp.sum(p, axis=-1, keepdims=True)
                outs.append(_dot(p, vw) / den)
                lses.append(m + jnp.log(den))
            o_ref[rows, cols] = jnp.where(lane < ATTN_HEAD_DIM, outs[0], outs[1])
            lse_ref[rows, cols] = jnp.where(lane < ATTN_HEAD_DIM, lses[0], lses[1])


def _band_attn(q, k, v, slopes, nseq, seq, dilation):
    n, d_attn = q.shape
    tq = _row_tile(seq, 512)
    nq = min(tq, 128)
    nt = seq // tq
    hb = tq // ATTN_RADIUS
    last = n // ATTN_RADIUS - 1
    cur = pl.BlockSpec((tq, d_attn), lambda s, i: (s * nt + i, 0))
    prev = pl.BlockSpec((ATTN_RADIUS, d_attn), lambda s, i: (jnp.maximum((s * nt + i) * hb - 1, 0), 0))
    nxt = pl.BlockSpec((ATTN_RADIUS, d_attn), lambda s, i: (jnp.minimum((s * nt + i + 1) * hb, last), 0))
    sds = jax.ShapeDtypeStruct((n, d_attn), F32)
    return pl.pallas_call(
        functools.partial(_band_attn_body, tq=tq, nq=nq, seq=seq, dilation=dilation, d_attn=d_attn),
        grid=(nseq, nt),
        in_specs=[pl.BlockSpec(memory_space=pltpu.SMEM), cur, cur, prev, nxt, cur, prev, nxt],
        out_specs=[cur, cur],
        out_shape=[sds, sds],
        compiler_params=_params("parallel", "parallel"),
    )(slopes, q, k, k, k, v, v, v)


def _attn_out_body(x_ref, o1_ref, o2_ref, o3_ref, l1_ref, l2_ref, l3_ref, w_ref, out_ref):
    l1, l2, l3 = l1_ref[...], l2_ref[...], l3_ref[...]
    m = jnp.maximum(jnp.maximum(l1, l2), l3)
    e1, e2, e3 = jnp.exp(l1 - m), jnp.exp(l2 - m), jnp.exp(l3 - m)
    o = (e1 * o1_ref[...] + e2 * o2_ref[...] + e3 * o3_ref[...]) / (e1 + e2 + e3)
    out_ref[...] = x_ref[...] + jnp.dot(o.astype(BF16), w_ref[...], preferred_element_type=F32)


def _attn_out(x, outs, lses, w_out):
    n, d = x.shape
    d_attn = w_out.shape[0]
    tm = _row_tile(n, 512)
    tok = lambda w: pl.BlockSpec((tm, w), lambda i: (i, 0))
    return pl.pallas_call(
        _attn_out_body,
        grid=(n // tm,),
        in_specs=[tok(d)] + [tok(d_attn)] * 6 + [pl.BlockSpec((d_attn, d), lambda i: (0, 0))],
        out_specs=tok(d),
        out_shape=jax.ShapeDtypeStruct((n, d), F32),
        compiler_params=_params("parallel"),
    )(x, *outs, *lses, w_out.astype(BF16))


def _dilated_attention(x, batch, seq, g, w_in, w_out, q_norm, k_norm):
    n, d = x.shape
    q, k, v = _attn_proj(x, g, w_in, q_norm, k_norm)
    d_attn = q.shape[1]
    heads = d_attn // ATTN_HEAD_DIM
    slopes = 2.0 ** (-8.0 * jnp.arange(1, heads + 1, dtype=F32) / heads)
    outs, lses = [], []
    for dil in DILATIONS:
        sub = seq // dil
        to_sub = lambda t: jnp.swapaxes(t.reshape(batch, sub, dil, d_attn), 1, 2).reshape(n, d_attn)
        from_sub = lambda t: jnp.swapaxes(t.reshape(batch, dil, sub, d_attn), 1, 2).reshape(n, d_attn)
        if dil == 1:
            o, lse = _band_attn(q, k, v, slopes, batch, seq, dil)
        else:
            o, lse = _band_attn(to_sub(q), to_sub(k), to_sub(v), slopes, batch * dil, sub, dil)
            o, lse = from_sub(o), from_sub(lse)
        outs.append(o)
        lses.append(lse)
    return _attn_out(x, outs, lses, w_out)


def _sgu_rwkv_mixer(x, batch, seq, g, w_in, w_out, sgu_norm, sgu_w, sgu_b, mu_rkv, mu_wa, w0, w_up,
                    a0, a_up, k_k, k_a, r_k, gn_w, gn_b):
    d_a = sgu_norm.shape[0]
    d_b = k_k.shape[0]
    ya, rkv, gate, wa = _proj_sgu(x, g, w_in, sgu_norm, sgu_w, sgu_b, d_a=d_a, d_b=d_b)
    yf, bonf, yb, bonb = _rwkv_scan(rkv, wa, batch, seq, mu_rkv, mu_wa, w0, w_up, a0, a_up,
                                    k_k, k_a, r_k.reshape(-1))
    return _rwkv_out(x, ya, yf, yb, bonf, bonb, gate, gn_w, gn_b, w_out)


def kernel(x_prompt, x_sample, ffn1_norm, ffn1_w_in, ffn1_w_out, mix_norm, ffn2_norm, ffn2_w_in,
           ffn2_w_out, block_norm, ab_w_in, ab_w_out, sgu_norm, sgu_w, sgu_b, rwkv_mu_rkv, rwkv_mu_wa,
           rwkv_w0, rwkv_w_up, rwkv_a0, rwkv_a_up, rwkv_k_k, rwkv_k_a, rwkv_r_k, rwkv_gn_w, rwkv_gn_b,
           attn_w_in, attn_w_out, attn_q_norm, attn_k_norm):
    depth = ffn1_norm.shape[0]

    def trunk(x3):
        batch, seq, d = x3.shape
        x = x3.reshape(batch * seq, d)
        for i in range(depth):
            j = i // 2
            x = _ffn(x, ffn1_norm[i], ffn1_w_in[i], ffn1_w_out[i], ffn1_norm[i], final_norm=False)
            if i % 2 == 0:
                x = _sgu_rwkv_mixer(x, batch, seq, mix_norm[i], ab_w_in[j], ab_w_out[j], sgu_norm[j],
                                    sgu_w[j], sgu_b[j], rwkv_mu_rkv[j], rwkv_mu_wa[j], rwkv_w0[j],
                                    rwkv_w_up[j], rwkv_a0[j], rwkv_a_up[j], rwkv_k_k[j], rwkv_k_a[j],
                                    rwkv_r_k[j], rwkv_gn_w[j], rwkv_gn_b[j])
            else:
                x = _dilated_attention(x, batch, seq, mix_norm[i], attn_w_in[j], attn_w_out[j],
                                       attn_q_norm[j], attn_k_norm[j])
            x = _ffn(x, ffn2_norm[i], ffn2_w_in[i], ffn2_w_out[i], block_norm[i], final_norm=True)
        return x.reshape(batch, seq, d)

    return trunk(x_prompt), trunk(x_sample)
```

```python
import functools
import math

import numpy as np
import jax
import jax.numpy as jnp
from jax import lax
from jax.experimental import pallas as pl
from jax.experimental.pallas import tpu as pltpu

F32 = jnp.float32
BF16 = jnp.bfloat16

NORM_EPS = 1e-6
SGU_GROUPS = 4
SGU_CHUNK = 128
RWKV_HEAD = 64
RWKV_LORA_W = 64
RWKV_GN_EPS = 64e-5
ATTN_HEAD_DIM = 64
ATTN_RADIUS = 64
DILATIONS = (1, 4, 16)
NEG_INF = -1e30

LANES = 128
MXU_TILE = 256
SCAN_CHUNK = 64
VMEM_LIMIT_BYTES = 52 * 1024 * 1024


def _params(*semantics):
    return pltpu.CompilerParams(dimension_semantics=semantics, vmem_limit_bytes=VMEM_LIMIT_BYTES)


def _dot(a, b):
    return jnp.dot(a.astype(BF16), b.astype(BF16), preferred_element_type=F32)


def _dot_nt(a, b):
    return lax.dot_general(a.astype(BF16), b.astype(BF16), (((1,), (1,)), ((), ())),
                           preferred_element_type=F32)


def _split(x):
    hi = x.astype(BF16)
    lo = (x - hi.astype(F32)).astype(BF16)
    return hi, lo


def _dot_exact_lhs(lhs_bf16, x):
    hi, lo = _split(x)
    return _dot(lhs_bf16, hi) + _dot(lhs_bf16, lo)


def _sigmoid(x):
    return 0.5 * jnp.tanh(0.5 * x) + 0.5


def _dot_exact_rhs(x, rhs_bf16):
    hi, lo = _split(x)
    return _dot(hi, rhs_bf16) + _dot(lo, rhs_bf16)


def _rms(x, g, eps=NORM_EPS):
    return x * lax.rsqrt(jnp.mean(x * x, axis=-1, keepdims=True) + eps) * g


def _row_tile(n, want):
    t = min(n, want)
    assert n % t == 0, (n, t)
    return t


def _ffn_body(x_ref, g_ref, wg_ref, wu_ref, wo_ref, g2_ref, o_ref, *, final_norm, f_splits):
    x = x_ref[...]
    h = _rms(x, g_ref[...]).astype(BF16)
    y = x
    for lo, hi in f_splits:
        gate = jnp.dot(h, wg_ref[:, lo:hi], preferred_element_type=F32)
        up = jnp.dot(h, wu_ref[:, lo:hi], preferred_element_type=F32)
        act = (gate * _sigmoid(gate) * up).astype(BF16)
        y = y + 0.5 * jnp.dot(act, wo_ref[lo:hi, :], preferred_element_type=F32)
    if final_norm:
        y = _rms(y, g2_ref[...])
    o_ref[...] = y


def _ffn(x, g, w_in, w_out, g2, *, final_norm):
    n, d = x.shape
    f_dim = w_out.shape[0]
    tm = _row_tile(n, 512)
    assert f_dim % MXU_TILE == 0
    mid = (f_dim // MXU_TILE // 2) * MXU_TILE
    f_splits = ((0, mid), (mid, f_dim))
    w_in = w_in.astype(BF16)
    w_out = w_out.astype(BF16)
    resident = lambda shape, imap: pl.BlockSpec(shape, imap, pipeline_mode=pl.Buffered(1))
    return pl.pallas_call(
        functools.partial(_ffn_body, final_norm=final_norm, f_splits=f_splits),
        grid=(n // tm,),
        in_specs=[
            pl.BlockSpec((tm, d), lambda i: (i, 0)),
            pl.BlockSpec((1, d), lambda i: (0, 0)),
            resident((d, f_dim), lambda i: (0, 0)),
            resident((d, f_dim), lambda i: (0, 1)),
            resident((f_dim, d), lambda i: (0, 0)),
            pl.BlockSpec((1, d), lambda i: (0, 0)),
        ],
        out_specs=pl.BlockSpec((tm, d), lambda i: (i, 0)),
        out_shape=jax.ShapeDtypeStruct((n, d), F32),
        compiler_params=_params("parallel"),
    )(x, g.reshape(1, d), w_in, w_in, w_out, g2.reshape(1, d))


def _proj_sgu_body(x_ref, g_ref, w_ref, sgn_ref, sw_ref, sb_ref, ya_ref, rkv_ref, gate_ref, wa_ref,
                   *, d_a, d_b):
    h = _rms(x_ref[...], g_ref[...]).astype(BF16)
    o_rkv, o_g, o_wa = 2 * d_a, 2 * d_a + 3 * d_b, 2 * d_a + 4 * d_b
    u = jnp.dot(h, w_ref[:, 0:d_a], preferred_element_type=F32)
    v = jnp.dot(h, w_ref[:, d_a:2 * d_a], preferred_element_type=F32)
    rkv_ref[...] = jnp.dot(h, w_ref[:, o_rkv:o_g], preferred_element_type=F32)
    gate_ref[...] = jnp.dot(h, w_ref[:, o_g:o_wa], preferred_element_type=F32)
    wa_ref[...] = jnp.dot(h, w_ref[:, o_wa:], preferred_element_type=F32)
    u = jax.nn.gelu(u)
    v = _rms(jax.nn.gelu(v), sgn_ref[...]).astype(BF16)
    gw = d_a // SGU_GROUPS
    for c in range(x_ref.shape[0] // SGU_CHUNK):
        rows = slice(c * SGU_CHUNK, (c + 1) * SGU_CHUNK)
        for grp in range(SGU_GROUPS):
            cols = slice(grp * gw, (grp + 1) * gw)
            mixed = jnp.dot(sw_ref[grp], v[rows, cols], preferred_element_type=F32) + sb_ref[:, cols]
            ya_ref[rows, cols] = (u[rows, cols] * mixed).astype(BF16)


def _proj_sgu(x, g, w_in, sgu_norm, sgu_w, sgu_b, *, d_a, d_b):
    n, d = x.shape
    d_in = w_in.shape[1]
    d_wa = d_in - 2 * d_a - 4 * d_b
    tm = _row_tile(n, 512)
    gw = d_a // SGU_GROUPS
    bias = jnp.repeat(sgu_b.T, gw, axis=1)
    return pl.pallas_call(
        functools.partial(_proj_sgu_body, d_a=d_a, d_b=d_b),
        grid=(n // tm,),
        in_specs=[
            pl.BlockSpec((tm, d), lambda i: (i, 0)),
            pl.BlockSpec((1, d), lambda i: (0, 0)),
            pl.BlockSpec((d, d_in), lambda i: (0, 0)),
            pl.BlockSpec((1, d_a), lambda i: (0, 0)),
            pl.BlockSpec((SGU_GROUPS, SGU_CHUNK, SGU_CHUNK), lambda i: (0, 0, 0)),
            pl.BlockSpec((SGU_CHUNK, d_a), lambda i: (0, 0)),
        ],
        out_specs=[
            pl.BlockSpec((tm, d_a), lambda i: (i, 0)),
            pl.BlockSpec((tm, 3 * d_b), lambda i: (i, 0)),
            pl.BlockSpec((tm, d_b), lambda i: (i, 0)),
            pl.BlockSpec((tm, d_wa), lambda i: (i, 0)),
        ],
        out_shape=[
            jax.ShapeDtypeStruct((n, d_a), BF16),
            jax.ShapeDtypeStruct((n, 3 * d_b), F32),
            jax.ShapeDtypeStruct((n, d_b), F32),
            jax.ShapeDtypeStruct((n, d_wa), F32),
        ],
        compiler_params=_params("parallel"),
    )(x, g.reshape(1, d), w_in.astype(BF16), sgu_norm.reshape(1, d_a), sgu_w.astype(BF16), bias)


def _rwkv_scan_body(rkvf_ref, hrf_ref, waf_ref, hwf_ref, rkvb_ref, hrb_ref, wab_ref, hwb_ref,
                    tri_ref, gsum_ref, mu_rkv_ref, mu_wa_ref, w0_ref, wup_ref, a0_ref, aup_ref,
                    kk_ref, ka_ref, rk_ref,
                    yf_ref, bonf_ref, yb_ref, bonb_ref,
                    s_ref, a_s, k_s, b_s, r_s, v_s, ae_s, ke_s, tot_s, *, tb, d_b):
    n = pl.program_id(1)
    nc = tb // SCAN_CHUNK
    assert RWKV_HEAD == SCAN_CHUNK and 2 * RWKV_HEAD == LANES
    first = n == 0
    row = lax.broadcasted_iota(jnp.int32, (tb, 1), 0)
    gsum = gsum_ref[...]

    @pl.when(first)
    def _():
        s_ref[...] = jnp.zeros_like(s_ref)

    def group_sum(x):
        return _dot(x, gsum)

    def prep(d, x, halo, wa, wa_halo, bon_ref):
        halo = jnp.where(first, 0.0, halo)
        wa_halo = jnp.where(first, 0.0, wa_halo)
        if d == 0:
            xs = jnp.where(row == 0, halo, pltpu.roll(x, 1, 0))
            was = jnp.where(row == 0, wa_halo, pltpu.roll(wa, 1, 0))
        else:
            xs = jnp.where(row == tb - 1, halo, pltpu.roll(x, tb - 1, 0))
            was = jnp.where(row == tb - 1, wa_halo, pltpu.roll(wa, tb - 1, 0))
        z = x + (xs - x) * mu_rkv_ref[d:d + 1, :]
        zwa = wa + (was - wa) * mu_wa_ref[d:d + 1, :]
        r, k, v = z[:, 0:d_b], z[:, d_b:2 * d_b], z[:, 2 * d_b:3 * d_b]
        w_raw = w0_ref[d:d + 1, :] + _dot(jnp.tanh(zwa), wup_ref[d])
        lw = -_sigmoid(w_raw) * math.exp(-0.5)
        a = _sigmoid(a0_ref[d:d + 1, :] + _dot(zwa, aup_ref[d]))
        kk = k * kk_ref[...]
        kk = kk * jnp.minimum(lax.rsqrt(group_sum(kk * kk)), 1e12)
        k2 = k * (1.0 + (a - 1.0) * ka_ref[...])
        bon_ref[...] = group_sum(r * k2 * rk_ref[...]) * v
        cums = _dot_exact_lhs(tri_ref[d], lw)
        cum, tot = cums[:tb], cums[tb:]
        e_neg = jnp.exp(-cum)
        e_end = jnp.exp(tot - cum)
        kka = kk * a
        a_s[d] = kka * e_neg
        k_s[d] = k2 * e_neg
        b_s[d] = kk * jnp.exp(cum - lw)
        r_s[d] = r * jnp.exp(cum)
        v_s[d] = v
        ae_s[d] = kka * e_end
        ke_s[d] = k2 * e_end
        tot_s[d] = tot

    prep(0, rkvf_ref[...], hrf_ref[7:8, :], waf_ref[...], hwf_ref[7:8, :], bonf_ref)
    prep(1, rkvb_ref[...], hrb_ref[0:1, :], wab_ref[...], hwb_ref[0:1, :], bonb_ref)

    y_refs = (yf_ref, yb_ref)
    cc_ = SCAN_CHUNK
    lane = lax.broadcasted_iota(jnp.int32, (1, LANES), 1)
    lo = lane < cc_
    row = lax.broadcasted_iota(jnp.int32, (cc_, LANES), 0)
    col = lax.broadcasted_iota(jnp.int32, (cc_, LANES), 1)
    tok = col & (cc_ - 1)
    diag_lo, diag_hi = col == row, col == row + cc_
    lane2 = lax.broadcasted_iota(jnp.int32, (1, 2 * LANES), 1)
    keep = (lane2 >= cc_, (lane2 < cc_) | (lane2 >= LANES))
    zeros1 = jnp.zeros((cc_, LANES), F32)
    zeros2 = jnp.zeros((cc_, 2 * LANES), F32)
    rows_cat = lambda *xs: jnp.concatenate(xs, axis=0)
    lanes_cat = lambda *xs: jnp.concatenate(xs, axis=1)

    def chunk_step(c, carry):
        pairs = []
        for d in range(2):
            r0 = pl.multiple_of((c if d == 0 else nc - 1 - c) * cc_, cc_)
            rows = pl.ds(r0, cc_)
            strict = (tok < row) if d == 0 else (tok > row)
            incl = (tok <= row) if d == 0 else (tok >= row)
            for p in range(d_b // LANES):
                pairs.append((d, p, rows, slice(p * LANES, (p + 1) * LANES), strict, incl))

        chains = []
        for d, p, rows, cols, strict, incl in pairs:
            bp, rp, vp = b_s[d, rows, cols], r_s[d, rows, cols], v_s[d, rows, cols]
            ap, kp = a_s[d, rows, cols], k_s[d, rows, cols]
            bx, rx = pltpu.roll(bp, cc_, 1), pltpu.roll(rp, cc_, 1)
            zv = lanes_cat(zeros1, vp)
            t = rows_cat(ae_s[d, rows, cols], ke_s[d, rows, cols]).T
            px = pltpu.roll(jnp.exp(tot_s[d, rows, cols]), cc_, 1)
            for odd in (False, True):
                chains.append(dict(d=d, p=p, odd=odd, mine=~lo if odd else lo, strict=strict, incl=incl,
                                   bp=bp, rp=rp, vp=vp, ap=ap, kp=kp, bx=bx, rx=rx, zv=zv, px=px,
                                   t=t[cc_:] if odd else t[:cc_]))

        for ch in chains:
            mine = ch["mine"]
            ch["g"] = _dot_nt(rows_cat(jnp.where(mine, ch["bp"], 0.0), jnp.where(mine, ch["rp"], 0.0)),
                              rows_cat(ch["kp"], ch["ap"]) if ch["odd"] else rows_cat(ch["ap"], ch["kp"]))
        for ch in chains:
            ch["gt"] = jnp.where(ch["strict"], ch["g"][:cc_], 0.0)
            ch["w"] = _dot(jnp.where(ch["mine"], 0.0, ch["gt"]),
                           rows_cat(ch["vp"], zeros1) if ch["odd"] else rows_cat(zeros1, ch["vp"]))
        for ch in chains:
            ch["s"] = lanes_cat(jnp.where(ch["mine"], -ch["gt"], ch["bx"]), ch["w"])

        for _ in range(6):
            for ch in chains:
                s = ch["s"]
                rhs = rows_cat(zeros2, s) if ch["odd"] else rows_cat(s, zeros2)
                ch["s"] = _dot(s[:, :LANES], rhs) + jnp.where(keep[ch["odd"]], s, 0.0)

        for ch in chains:
            s, zv = ch["s"], ch["zv"]
            ch["ry"] = _dot(jnp.where(ch["incl"], ch["g"][cc_:], 0.0),
                            rows_cat(zv, -s) if ch["odd"] else rows_cat(-s, zv))
        for ch in chains:
            ch["pp"] = _dot(ch["t"], rows_cat(ch["s"], -ch["zv"]))
        outs = {}
        for ch in chains:
            d, p, odd, ry, pp = ch["d"], ch["p"], ch["odd"], ch["ry"], ch["pp"]
            phi = jnp.where(diag_lo if odd else diag_hi, ch["px"], 0.0) - pp[:, :LANES]
            lhs = rows_cat(ch["rx"] + ry[:, :LANES], phi)
            h0 = s_ref[d, p]
            out = _dot(lhs, rows_cat(h0, zeros1) if odd else rows_cat(zeros1, h0))
            outs[(d, p, odd)] = out + rows_cat(ry[:, LANES:], -pp[:, LANES:])
        for d, p, rows, cols, strict, incl in pairs:
            res = jnp.where(lo, outs[(d, p, False)], outs[(d, p, True)])
            y_refs[d][rows, cols] = res[:cc_]
            s_ref[d, p] = res[cc_:]
        return carry

    lax.fori_loop(0, nc, chunk_step, 0)


def _rwkv_scan(rkv, wa, batch, seq, mu_rkv, mu_wa, w0, w_up, a0, a_up, k_k, k_a, r_k):
    n, d3 = rkv.shape
    d_b = d3 // 3
    lora = wa.shape[1] // 2
    tb = _row_tile(seq, 256)
    nb = seq // tb
    hb = tb // 8
    last8 = n // 8 - 1

    pos = np.arange(tb)
    same = (pos[:, None] // SCAN_CHUNK) == (pos[None, :] // SCAN_CHUNK)
    tri = np.stack([np.concatenate([same & (pos[None, :] <= pos[:, None]), same]),
                    np.concatenate([same & (pos[None, :] >= pos[:, None]), same])])
    tri = jnp.asarray(tri, BF16)
    ch = np.arange(d_b) // RWKV_HEAD
    gsum = jnp.asarray(ch[:, None] == ch[None, :], BF16)
    wup = jnp.concatenate([w_up, jnp.zeros((2, lora - w_up.shape[1], d_b), F32)], axis=1).astype(BF16)
    aup = jnp.concatenate([jnp.zeros((2, lora - a_up.shape[1], d_b), F32), a_up], axis=1).astype(BF16)

    fwd = lambda b, i: (b * nb + i, 0)
    bwd = lambda b, i: (b * nb + nb - 1 - i, 0)
    fwd_halo = lambda b, i: (jnp.maximum((b * nb + i) * hb - 1, 0), 0)
    bwd_halo = lambda b, i: (jnp.minimum((b * nb + nb - i) * hb, last8), 0)
    const2 = lambda b, i: (0, 0)
    const3 = lambda b, i: (0, 0, 0)
    out_spec_f = pl.BlockSpec((tb, d_b), fwd)
    out_spec_b = pl.BlockSpec((tb, d_b), bwd)
    out_sds = jax.ShapeDtypeStruct((n, d_b), F32)
    return pl.pallas_call(
        functools.partial(_rwkv_scan_body, tb=tb, d_b=d_b),
        grid=(batch, nb),
        in_specs=[
            pl.BlockSpec((tb, d3), fwd), pl.BlockSpec((8, d3), fwd_halo),
            pl.BlockSpec((tb, lora), fwd), pl.BlockSpec((8, lora), fwd_halo),
            pl.BlockSpec((tb, d3), bwd), pl.BlockSpec((8, d3), bwd_halo),
            pl.BlockSpec((tb, lora), lambda b, i: (b * nb + nb - 1 - i, 1)),
            pl.BlockSpec((8, lora), lambda b, i: (jnp.minimum((b * nb + nb - i) * hb, last8), 1)),
            pl.BlockSpec((2, 2 * tb, tb), const3),
            pl.BlockSpec((d_b, d_b), const2),
            pl.BlockSpec((2, d3), const2),
            pl.BlockSpec((2, lora), const2),
            pl.BlockSpec((2, d_b), const2),
            pl.BlockSpec((2, lora, d_b), const3),
            pl.BlockSpec((2, d_b), const2),
            pl.BlockSpec((2, lora, d_b), const3),
            pl.BlockSpec((1, d_b), const2),
            pl.BlockSpec((1, d_b), const2),
            pl.BlockSpec((1, d_b), const2),
        ],
        out_specs=[out_spec_f, out_spec_f, out_spec_b, out_spec_b],
        out_shape=[out_sds] * 4,
        scratch_shapes=[pltpu.VMEM((2, d_b // LANES, RWKV_HEAD, LANES), F32)]
        + [pltpu.VMEM((2, tb, d_b), F32)] * 8,
        compiler_params=_params("parallel", "arbitrary"),
    )(rkv, rkv, wa, wa, rkv, rkv, wa, wa, tri, gsum, mu_rkv, mu_wa, w0, wup, a0, aup,
      k_k.reshape(1, d_b), k_a.reshape(1, d_b), r_k.reshape(1, d_b))


def _rwkv_out_body(x_ref, ya_ref, yf_ref, yb_ref, bonf_ref, bonb_ref, gate_ref, gmean_ref,
                   gnw_ref, gnb_ref, woa_ref, wob_ref, o_ref):
    gmean = gmean_ref[...]
    y = yf_ref[...] + yb_ref[...]
    dev = y - _dot_exact_rhs(y, gmean)
    var = _dot_exact_rhs(dev * dev, gmean)
    y = dev * lax.rsqrt(var + RWKV_GN_EPS) * gnw_ref[...] + gnb_ref[...]
    y = (y + (bonf_ref[...] + bonb_ref[...])) * _sigmoid(gate_ref[...])
    o_ref[...] = (x_ref[...] + jnp.dot(ya_ref[...], woa_ref[...], preferred_element_type=F32)
                  + jnp.dot(y.astype(BF16), wob_ref[...], preferred_element_type=F32))


def _rwkv_out(x, ya, yf, yb, bonf, bonb, gate, gn_w, gn_b, w_out):
    n, d = x.shape
    d_a, d_b = ya.shape[1], yf.shape[1]
    tm = _row_tile(n, 512)
    ch = np.arange(d_b) // RWKV_HEAD
    gmean = jnp.asarray((ch[:, None] == ch[None, :]) / RWKV_HEAD, BF16)
    w_out = w_out.astype(BF16)
    tok = lambda w: pl.BlockSpec((tm, w), lambda i: (i, 0))
    const = lambda r, c: pl.BlockSpec((r, c), lambda i: (0, 0))
    return pl.pallas_call(
        _rwkv_out_body,
        grid=(n // tm,),
        in_specs=[tok(d), tok(d_a), tok(d_b), tok(d_b), tok(d_b), tok(d_b), tok(d_b),
                  const(d_b, d_b), const(1, d_b), const(1, d_b), const(d_a, d), const(d_b, d)],
        out_specs=tok(d),
        out_shape=jax.ShapeDtypeStruct((n, d), F32),
        compiler_params=_params("parallel"),
    )(x, ya, yf, yb, bonf, bonb, gate, gmean, gn_w.reshape(1, d_b), gn_b.reshape(1, d_b),
      w_out[:d_a], w_out[d_a:])


def _attn_proj_body(x_ref, g_ref, w_ref, hmean_ref, qn_ref, kn_ref, q_ref, k_ref, v_ref, *, d_attn):
    h = _rms(x_ref[...], g_ref[...]).astype(BF16)
    hmean = hmean_ref[...]

    def head_rms(t, gain):
        ms = jnp.concatenate(
            [_dot_exact_rhs(t[:, c:c + LANES] * t[:, c:c + LANES], hmean) for c in range(0, d_attn, LANES)],
            axis=1)
        return t * lax.rsqrt(ms + NORM_EPS) * gain

    q = jnp.dot(h, w_ref[:, 0:d_attn], preferred_element_type=F32)
    q_ref[...] = (head_rms(q, qn_ref[...]) * (ATTN_HEAD_DIM ** -0.5)).astype(BF16)
    k = jnp.dot(h, w_ref[:, d_attn:2 * d_attn], preferred_element_type=F32)
    k_ref[...] = head_rms(k, kn_ref[...]).astype(BF16)
    v_ref[...] = jnp.dot(h, w_ref[:, 2 * d_attn:], preferred_element_type=F32).astype(BF16)


def _attn_proj(x, g, w_in, q_norm, k_norm):
    n, d = x.shape
    d_attn = w_in.shape[1] // 3
    heads = d_attn // ATTN_HEAD_DIM
    tm = _row_tile(n, 512)
    ch = np.arange(LANES) // ATTN_HEAD_DIM
    hmean = jnp.asarray((ch[:, None] == ch[None, :]) / ATTN_HEAD_DIM, BF16)
    tok = pl.BlockSpec((tm, d_attn), lambda i: (i, 0))
    const = lambda r, c: pl.BlockSpec((r, c), lambda i: (0, 0))
    sds = jax.ShapeDtypeStruct((n, d_attn), BF16)
    return pl.pallas_call(
        functools.partial(_attn_proj_body, d_attn=d_attn),
        grid=(n // tm,),
        in_specs=[pl.BlockSpec((tm, d), lambda i: (i, 0)), const(1, d), const(d, 3 * d_attn),
                  const(LANES, LANES), const(1, d_attn), const(1, d_attn)],
        out_specs=[tok, tok, tok],
        out_shape=[sds, sds, sds],
        compiler_params=_params("parallel"),
    )(x, g.reshape(1, d), w_in.astype(BF16), hmean,
      jnp.tile(q_norm, heads).reshape(1, d_attn), jnp.tile(k_norm, heads).reshape(1, d_attn))


def _band_attn_body(slopes_ref, q_ref, kc_ref, kp_ref, kn_ref, vc_ref, vp_ref, vn_ref, o_ref, lse_ref,
                    *, tq, nq, seq, dilation, d_attn):
    t0 = pl.program_id(1) * tq
    nk = nq + 2 * ATTN_RADIUS
    lane = lax.broadcasted_iota(jnp.int32, (1, LANES), 1)
    qi = lax.broadcasted_iota(jnp.int32, (nq, nk), 0)
    kj = lax.broadcasted_iota(jnp.int32, (nq, nk), 1) - ATTN_RADIUS
    dist = jnp.abs(qi - kj)
    dist_f = (dist * dilation).astype(F32)
    in_band = dist <= ATTN_RADIUS
    for hp in range(d_attn // LANES):
        cols = slice(hp * LANES, (hp + 1) * LANES)
        kfull = jnp.concatenate([kp_ref[:, cols], kc_ref[:, cols], kn_ref[:, cols]], axis=0)
        vfull = jnp.concatenate([vp_ref[:, cols], vc_ref[:, cols], vn_ref[:, cols]], axis=0)
        for sb in range(tq // nq):
            rows = slice(sb * nq, (sb + 1) * nq)
            qs = q_ref[rows, cols]
            kw = kfull[sb * nq:sb * nq + nk]
            vw = vfull[sb * nq:sb * nq + nk]
            kpos = t0 + sb * nq + kj
            valid = in_band & (kpos >= 0) & (kpos < seq)
            outs, lses = [], []
            for hh in range(LANES // ATTN_HEAD_DIM):
                in_head = (lane // ATTN_HEAD_DIM) == hh
                slope = slopes_ref[hp * (LANES // ATTN_HEAD_DIM) + hh]
                s = _dot_nt(jnp.where(in_head, qs, jnp.zeros_like(qs)), kw)
                s = jnp.where(valid, s - slope * dist_f, NEG_INF)
                m = jnp.max(s, axis=-1, keepdims=True)
                p = jnp.exp(s - m)
                den = jnp.sum(p, axis=-1, keepdims=True)
                outs.append(_dot(p, vw) / den)
                lses.append(m + jnp.log(den))
            o_ref[rows, cols] = jnp.where(lane < ATTN_HEAD_DIM, outs[0], outs[1])
            lse_ref[rows, cols] = jnp.where(lane < ATTN_HEAD_DIM, lses[0], lses[1])


def _band_attn(q, k, v, slopes, nseq, seq, dilation):
    n, d_attn = q.shape
    tq = _row_tile(seq, 512)
    nq = min(tq, 128)
    nt = seq // tq
    hb = tq // ATTN_RADIUS
    last = n // ATTN_RADIUS - 1
    cur = pl.BlockSpec((tq, d_attn), lambda s, i: (s * nt + i, 0))
    prev = pl.BlockSpec((ATTN_RADIUS, d_attn), lambda s, i: (jnp.maximum((s * nt + i) * hb - 1, 0), 0))
    nxt = pl.BlockSpec((ATTN_RADIUS, d_attn), lambda s, i: (jnp.minimum((s * nt + i + 1) * hb, last), 0))
    sds = jax.ShapeDtypeStruct((n, d_attn), F32)
    return pl.pallas_call(
        functools.partial(_band_attn_body, tq=tq, nq=nq, seq=seq, dilation=dilation, d_attn=d_attn),
        grid=(nseq, nt),
        in_specs=[pl.BlockSpec(memory_space=pltpu.SMEM), cur, cur, prev, nxt, cur, prev, nxt],
        out_specs=[cur, cur],
        out_shape=[sds, sds],
        compiler_params=_params("parallel", "parallel"),
    )(slopes, q, k, k, k, v, v, v)


def _attn_out_body(x_ref, o1_ref, o2_ref, o3_ref, l1_ref, l2_ref, l3_ref, w_ref, out_ref):
    l1, l2, l3 = l1_ref[...], l2_ref[...], l3_ref[...]
    m = jnp.maximum(jnp.maximum(l1, l2), l3)
    e1, e2, e3 = jnp.exp(l1 - m), jnp.exp(l2 - m), jnp.exp(l3 - m)
    o = (e1 * o1_ref[...] + e2 * o2_ref[...] + e3 * o3_ref[...]) / (e1 + e2 + e3)
    out_ref[...] = x_ref[...] + jnp.dot(o.astype(BF16), w_ref[...], preferred_element_type=F32)


def _attn_out(x, outs, lses, w_out):
    n, d = x.shape
    d_attn = w_out.shape[0]
    tm = _row_tile(n, 512)
    tok = lambda w: pl.BlockSpec((tm, w), lambda i: (i, 0))
    return pl.pallas_call(
        _attn_out_body,
        grid=(n // tm,),
        in_specs=[tok(d)] + [tok(d_attn)] * 6 + [pl.BlockSpec((d_attn, d), lambda i: (0, 0))],
        out_specs=tok(d),
        out_shape=jax.ShapeDtypeStruct((n, d), F32),
        compiler_params=_params("parallel"),
    )(x, *outs, *lses, w_out.astype(BF16))


def _dilated_attention(x, batch, seq, g, w_in, w_out, q_norm, k_norm):
    n, d = x.shape
    q, k, v = _attn_proj(x, g, w_in, q_norm, k_norm)
    d_attn = q.shape[1]
    heads = d_attn // ATTN_HEAD_DIM
    slopes = 2.0 ** (-8.0 * jnp.arange(1, heads + 1, dtype=F32) / heads)
    outs, lses = [], []
    for dil in DILATIONS:
        sub = seq // dil
        to_sub = lambda t: jnp.swapaxes(t.reshape(batch, sub, dil, d_attn), 1, 2).reshape(n, d_attn)
        from_sub = lambda t: jnp.swapaxes(t.reshape(batch, dil, sub, d_attn), 1, 2).reshape(n, d_attn)
        if dil == 1:
            o, lse = _band_attn(q, k, v, slopes, batch, seq, dil)
        else:
            o, lse = _band_attn(to_sub(q), to_sub(k), to_sub(v), slopes, batch * dil, sub, dil)
            o, lse = from_sub(o), from_sub(lse)
        outs.append(o)
        lses.append(lse)
    return _attn_out(x, outs, lses, w_out)


def _sgu_rwkv_mixer(x, batch, seq, g, w_in, w_out, sgu_norm, sgu_w, sgu_b, mu_rkv, mu_wa, w0, w_up,
                    a0, a_up, k_k, k_a, r_k, gn_w, gn_b):
    d_a = sgu_norm.shape[0]
    d_b = k_k.shape[0]
    ya, rkv, gate, wa = _proj_sgu(x, g, w_in, sgu_norm, sgu_w, sgu_b, d_a=d_a, d_b=d_b)
    yf, bonf, yb, bonb = _rwkv_scan(rkv, wa, batch, seq, mu_rkv, mu_wa, w0, w_up, a0, a_up,
                                    k_k, k_a, r_k.reshape(-1))
    return _rwkv_out(x, ya, yf, yb, bonf, bonb, gate, gn_w, gn_b, w_out)


def kernel(x_prompt, x_sample, ffn1_norm, ffn1_w_in, ffn1_w_out, mix_norm, ffn2_norm, ffn2_w_in,
           ffn2_w_out, block_norm, ab_w_in, ab_w_out, sgu_norm, sgu_w, sgu_b, rwkv_mu_rkv, rwkv_mu_wa,
           rwkv_w0, rwkv_w_up, rwkv_a0, rwkv_a_up, rwkv_k_k, rwkv_k_a, rwkv_r_k, rwkv_gn_w, rwkv_gn_b,
           attn_w_in, attn_w_out, attn_q_norm, attn_k_norm):
    depth = ffn1_norm.shape[0]

    def trunk(x3):
        batch, seq, d = x3.shape
        x = x3.reshape(batch * seq, d)
        for i in range(depth):
            j = i // 2
            x = _ffn(x, ffn1_norm[i], ffn1_w_in[i], ffn1_w_out[i], ffn1_norm[i], final_norm=False)
            if i % 2 == 0:
                x = _sgu_rwkv_mixer(x, batch, seq, mix_norm[i], ab_w_in[j], ab_w_out[j], sgu_norm[j],
                                    sgu_w[j], sgu_b[j], rwkv_mu_rkv[j], rwkv_mu_wa[j], rwkv_w0[j],
                                    rwkv_w_up[j], rwkv_a0[j], rwkv_a_up[j], rwkv_k_k[j], rwkv_k_a[j],
                                    rwkv_r_k[j], rwkv_gn_w[j], rwkv_gn_b[j])
            else:
                x = _dilated_attention(x, batch, seq, mix_norm[i], attn_w_in[j], attn_w_out[j],
                                       attn_q_norm[j], attn_k_norm[j])
            x = _ffn(x, ffn2_norm[i], ffn2_w_in[i], ffn2_w_out[i], block_norm[i], final_norm=True)
        return x.reshape(batch, seq, d)

    return trunk(x_prompt), trunk(x_sample)
```

```python
import functools
import math

import numpy as np
import jax
import jax.numpy as jnp
from jax import lax
from jax.experimental import pallas as pl
from jax.experimental.pallas import tpu as pltpu

F32 = jnp.float32
BF16 = jnp.bfloat16

NORM_EPS = 1e-6
SGU_GROUPS = 4
SGU_CHUNK = 128
RWKV_HEAD = 64
RWKV_LORA_W = 64
RWKV_GN_EPS = 64e-5
ATTN_HEAD_DIM = 64
ATTN_RADIUS = 64
DILATIONS = (1, 4, 16)
NEG_INF = -1e30

LANES = 128
MXU_TILE = 256
SCAN_CHUNK = 64
VMEM_LIMIT_BYTES = 52 * 1024 * 1024


def _params(*semantics):
    return pltpu.CompilerParams(dimension_semantics=semantics, vmem_limit_bytes=VMEM_LIMIT_BYTES)


def _dot(a, b):
    return jnp.dot(a.astype(BF16), b.astype(BF16), preferred_element_type=F32)


def _dot_nt(a, b):
    return lax.dot_general(a.astype(BF16), b.astype(BF16), (((1,), (1,)), ((), ())),
                           preferred_element_type=F32)


def _split(x):
    hi = x.astype(BF16)
    lo = (x - hi.astype(F32)).astype(BF16)
    return hi, lo


def _dot_exact_lhs(lhs_bf16, x):
    hi, lo = _split(x)
    return _dot(lhs_bf16, hi) + _dot(lhs_bf16, lo)


def _sigmoid(x):
    return 0.5 * jnp.tanh(0.5 * x) + 0.5


def _dot_exact_rhs(x, rhs_bf16):
    hi, lo = _split(x)
    return _dot(hi, rhs_bf16) + _dot(lo, rhs_bf16)


def _rms(x, g, eps=NORM_EPS):
    return x * lax.rsqrt(jnp.mean(x * x, axis=-1, keepdims=True) + eps) * g


def _row_tile(n, want):
    t = min(n, want)
    assert n % t == 0, (n, t)
    return t


def _ffn_body(x_ref, g_ref, wg_ref, wu_ref, wo_ref, g2_ref, o_ref, *, final_norm, f_splits):
    x = x_ref[...]
    h = _rms(x, g_ref[...]).astype(BF16)
    y = x
    for lo, hi in f_splits:
        gate = jnp.dot(h, wg_ref[:, lo:hi], preferred_element_type=F32)
        up = jnp.dot(h, wu_ref[:, lo:hi], preferred_element_type=F32)
        act = (gate * _sigmoid(gate) * up).astype(BF16)
        y = y + 0.5 * jnp.dot(act, wo_ref[lo:hi, :], preferred_element_type=F32)
    if final_norm:
        y = _rms(y, g2_ref[...])
    o_ref[...] = y


def _ffn(x, g, w_in, w_out, g2, *, final_norm):
    n, d = x.shape
    f_dim = w_out.shape[0]
    tm = _row_tile(n, 512)
    assert f_dim % MXU_TILE == 0
    mid = (f_dim // MXU_TILE // 2) * MXU_TILE
    f_splits = ((0, mid), (mid, f_dim))
    w_in = w_in.astype(BF16)
    w_out = w_out.astype(BF16)
    resident = lambda shape, imap: pl.BlockSpec(shape, imap, pipeline_mode=pl.Buffered(1))
    return pl.pallas_call(
        functools.partial(_ffn_body, final_norm=final_norm, f_splits=f_splits),
        grid=(n // tm,),
        in_specs=[
            pl.BlockSpec((tm, d), lambda i: (i, 0)),
            pl.BlockSpec((1, d), lambda i: (0, 0)),
            resident((d, f_dim), lambda i: (0, 0)),
            resident((d, f_dim), lambda i: (0, 1)),
            resident((f_dim, d), lambda i: (0, 0)),
            pl.BlockSpec((1, d), lambda i: (0, 0)),
        ],
        out_specs=pl.BlockSpec((tm, d), lambda i: (i, 0)),
        out_shape=jax.ShapeDtypeStruct((n, d), F32),
        compiler_params=_params("parallel"),
    )(x, g.reshape(1, d), w_in, w_in, w_out, g2.reshape(1, d))


def _proj_sgu_body(x_ref, g_ref, w_ref, sgn_ref, sw_ref, sb_ref, ya_ref, rkv_ref, gate_ref, wa_ref,
                   *, d_a, d_b):
    h = _rms(x_ref[...], g_ref[...]).astype(BF16)
    o_rkv, o_g, o_wa = 2 * d_a, 2 * d_a + 3 * d_b, 2 * d_a + 4 * d_b
    u = jnp.dot(h, w_ref[:, 0:d_a], preferred_element_type=F32)
    v = jnp.dot(h, w_ref[:, d_a:2 * d_a], preferred_element_type=F32)
    rkv_ref[...] = jnp.dot(h, w_ref[:, o_rkv:o_g], preferred_element_type=F32)
    gate_ref[...] = jnp.dot(h, w_ref[:, o_g:o_wa], preferred_element_type=F32)
    wa_ref[...] = jnp.dot(h, w_ref[:, o_wa:], preferred_element_type=F32)
    u = jax.nn.gelu(u)
    v = _rms(jax.nn.gelu(v), sgn_ref[...]).astype(BF16)
    gw = d_a // SGU_GROUPS
    for c in range(x_ref.shape[0] // SGU_CHUNK):
        rows = slice(c * SGU_CHUNK, (c + 1) * SGU_CHUNK)
        for grp in range(SGU_GROUPS):
            cols = slice(grp * gw, (grp + 1) * gw)
            mixed = jnp.dot(sw_ref[grp], v[rows, cols], preferred_element_type=F32) + sb_ref[:, cols]
            ya_ref[rows, cols] = (u[rows, cols] * mixed).astype(BF16)


def _proj_sgu(x, g, w_in, sgu_norm, sgu_w, sgu_b, *, d_a, d_b):
    n, d = x.shape
    d_in = w_in.shape[1]
    d_wa = d_in - 2 * d_a - 4 * d_b
    tm = _row_tile(n, 512)
    gw = d_a // SGU_GROUPS
    bias = jnp.repeat(sgu_b.T, gw, axis=1)
    return pl.pallas_call(
        functools.partial(_proj_sgu_body, d_a=d_a, d_b=d_b),
        grid=(n // tm,),
        in_specs=[
            pl.BlockSpec((tm, d), lambda i: (i, 0)),
            pl.BlockSpec((1, d), lambda i: (0, 0)),
            pl.BlockSpec((d, d_in), lambda i: (0, 0)),
            pl.BlockSpec((1, d_a), lambda i: (0, 0)),
            pl.BlockSpec((SGU_GROUPS, SGU_CHUNK, SGU_CHUNK), lambda i: (0, 0, 0)),
            pl.BlockSpec((SGU_CHUNK, d_a), lambda i: (0, 0)),
        ],
        out_specs=[
            pl.BlockSpec((tm, d_a), lambda i: (i, 0)),
            pl.BlockSpec((tm, 3 * d_b), lambda i: (i, 0)),
            pl.BlockSpec((tm, d_b), lambda i: (i, 0)),
            pl.BlockSpec((tm, d_wa), lambda i: (i, 0)),
        ],
        out_shape=[
            jax.ShapeDtypeStruct((n, d_a), BF16),
            jax.ShapeDtypeStruct((n, 3 * d_b), F32),
            jax.ShapeDtypeStruct((n, d_b), F32),
            jax.ShapeDtypeStruct((n, d_wa), F32),
        ],
        compiler_params=_params("parallel"),
    )(x, g.reshape(1, d), w_in.astype(BF16), sgu_norm.reshape(1, d_a), sgu_w.astype(BF16), bias)


def _rwkv_scan_body(rkvf_ref, hrf_ref, waf_ref, hwf_ref, rkvb_ref, hrb_ref, wab_ref, hwb_ref,
                    tri_ref, gsum_ref, mu_rkv_ref, mu_wa_ref, w0_ref, wup_ref, a0_ref, aup_ref,
                    kk_ref, ka_ref, rk_ref,
                    yf_ref, bonf_ref, yb_ref, bonb_ref,
                    s_ref, a_s, k_s, b_s, r_s, v_s, ae_s, ke_s, tot_s, *, tb, d_b):
    n = pl.program_id(1)
    nc = tb // SCAN_CHUNK
    assert RWKV_HEAD == SCAN_CHUNK and 2 * RWKV_HEAD == LANES
    first = n == 0
    row = lax.broadcasted_iota(jnp.int32, (tb, 1), 0)
    gsum = gsum_ref[...]

    @pl.when(first)
    def _():
        s_ref[...] = jnp.zeros_like(s_ref)

    def group_sum(x):
        return _dot(x, gsum)

    def prep(d, x, halo, wa, wa_halo, bon_ref):
        halo = jnp.where(first, 0.0, halo)
        wa_halo = jnp.where(first, 0.0, wa_halo)
        if d == 0:
            xs = jnp.where(row == 0, halo, pltpu.roll(x, 1, 0))
            was = jnp.where(row == 0, wa_halo, pltpu.roll(wa, 1, 0))
        else:
            xs = jnp.where(row == tb - 1, halo, pltpu.roll(x, tb - 1, 0))
            was = jnp.where(row == tb - 1, wa_halo, pltpu.roll(wa, tb - 1, 0))
        z = x + (xs - x) * mu_rkv_ref[d:d + 1, :]
        zwa = wa + (was - wa) * mu_wa_ref[d:d + 1, :]
        r, k, v = z[:, 0:d_b], z[:, d_b:2 * d_b], z[:, 2 * d_b:3 * d_b]
        w_raw = w0_ref[d:d + 1, :] + _dot(jnp.tanh(zwa), wup_ref[d])
        lw = -_sigmoid(w_raw) * math.exp(-0.5)
        a = _sigmoid(a0_ref[d:d + 1, :] + _dot(zwa, aup_ref[d]))
        kk = k * kk_ref[...]
        kk = kk * jnp.minimum(lax.rsqrt(group_sum(kk * kk)), 1e12)
        k2 = k * (1.0 + (a - 1.0) * ka_ref[...])
        bon_ref[...] = group_sum(r * k2 * rk_ref[...]) * v
        cums = _dot_exact_lhs(tri_ref[d], lw)
        cum, tot = cums[:tb], cums[tb:]
        e_neg = jnp.exp(-cum)
        e_end = jnp.exp(tot - cum)
        kka = kk * a
        a_s[d] = kka * e_neg
        k_s[d] = k2 * e_neg
        b_s[d] = kk * jnp.exp(cum - lw)
        r_s[d] = r * jnp.exp(cum)
        v_s[d] = v
        ae_s[d] = kka * e_end
        ke_s[d] = k2 * e_end
        tot_s[d] = tot

    prep(0, rkvf_ref[...], hrf_ref[7:8, :], waf_ref[...], hwf_ref[7:8, :], bonf_ref)
    prep(1, rkvb_ref[...], hrb_ref[0:1, :], wab_ref[...], hwb_ref[0:1, :], bonb_ref)

    y_refs = (yf_ref, yb_ref)
    cc_ = SCAN_CHUNK
    lane = lax.broadcasted_iota(jnp.int32, (1, LANES), 1)
    lo = lane < cc_
    row = lax.broadcasted_iota(jnp.int32, (cc_, LANES), 0)
    col = lax.broadcasted_iota(jnp.int32, (cc_, LANES), 1)
    tok = col & (cc_ - 1)
    diag_lo, diag_hi = col == row, col == row + cc_
    lane2 = lax.broadcasted_iota(jnp.int32, (1, 2 * LANES), 1)
    keep = (lane2 >= cc_, (lane2 < cc_) | (lane2 >= LANES))
    zeros1 = jnp.zeros((cc_, LANES), F32)
    zeros2 = jnp.zeros((cc_, 2 * LANES), F32)
    rows_cat = lambda *xs: jnp.concatenate(xs, axis=0)
    lanes_cat = lambda *xs: jnp.concatenate(xs, axis=1)

    def chunk_step(c, carry):
        pairs = []
        for d in range(2):
            r0 = pl.multiple_of((c if d == 0 else nc - 1 - c) * cc_, cc_)
            rows = pl.ds(r0, cc_)
            strict = (tok < row) if d == 0 else (tok > row)
            incl = (tok <= row) if d == 0 else (tok >= row)
            for p in range(d_b // LANES):
                pairs.append((d, p, rows, slice(p * LANES, (p + 1) * LANES), strict, incl))

        chains = []
        for d, p, rows, cols, strict, incl in pairs:
            bp, rp, vp = b_s[d, rows, cols], r_s[d, rows, cols], v_s[d, rows, cols]
            ap, kp = a_s[d, rows, cols], k_s[d, rows, cols]
            bx, rx = pltpu.roll(bp, cc_, 1), pltpu.roll(rp, cc_, 1)
            zv = lanes_cat(zeros1, vp)
            t = rows_cat(ae_s[d, rows, cols], ke_s[d, rows, cols]).T
            px = pltpu.roll(jnp.exp(tot_s[d, rows, cols]), cc_, 1)
            for odd in (False, True):
                chains.append(dict(d=d, p=p, odd=odd, mine=~lo if odd else lo, strict=strict, incl=incl,
                                   bp=bp, rp=rp, vp=vp, ap=ap, kp=kp, bx=bx, rx=rx, zv=zv, px=px,
                                   t=t[cc_:] if odd else t[:cc_]))

        for ch in chains:
            mine = ch["mine"]
            ch["g"] = _dot_nt(rows_cat(jnp.where(mine, ch["bp"], 0.0), jnp.where(mine, ch["rp"], 0.0)),
                              rows_cat(ch["kp"], ch["ap"]) if ch["odd"] else rows_cat(ch["ap"], ch["kp"]))
        for ch in chains:
            ch["gt"] = jnp.where(ch["strict"], ch["g"][:cc_], 0.0)
            ch["w"] = _dot(jnp.where(ch["mine"], 0.0, ch["gt"]),
                           rows_cat(ch["vp"], zeros1) if ch["odd"] else rows_cat(zeros1, ch["vp"]))
        for ch in chains:
            ch["s"] = lanes_cat(jnp.where(ch["mine"], -ch["gt"], ch["bx"]), ch["w"])

        for _ in range(6):
            for ch in chains:
                s = ch["s"]
                rhs = rows_cat(zeros2, s) if ch["odd"] else rows_cat(s, zeros2)
                ch["s"] = _dot(s[:, :LANES], rhs) + jnp.where(keep[ch["odd"]], s, 0.0)

        for ch in chains:
            s, zv = ch["s"], ch["zv"]
            ch["ry"] = _dot(jnp.where(ch["incl"], ch["g"][cc_:], 0.0),
                            rows_cat(zv, -s) if ch["odd"] else rows_cat(-s, zv))
        for ch in chains:
            ch["pp"] = _dot(ch["t"], rows_cat(ch["s"], -ch["zv"]))
        outs = {}
        for ch in chains:
            d, p, odd, ry, pp = ch["d"], ch["p"], ch["odd"], ch["ry"], ch["pp"]
            phi = jnp.where(diag_lo if odd else diag_hi, ch["px"], 0.0) - pp[:, :LANES]
            lhs = rows_cat(ch["rx"] + ry[:, :LANES], phi)
            h0 = s_ref[d, p]
            out = _dot(lhs, rows_cat(h0, zeros1) if odd else rows_cat(zeros1, h0))
            outs[(d, p, odd)] = out + rows_cat(ry[:, LANES:], -pp[:, LANES:])
        for d, p, rows, cols, strict, incl in pairs:
            res = jnp.where(lo, outs[(d, p, False)], outs[(d, p, True)])
            y_refs[d][rows, cols] = res[:cc_]
            s_ref[d, p] = res[cc_:]
        return carry

    lax.fori_loop(0, nc, chunk_step, 0)


def _rwkv_scan(rkv, wa, batch, seq, mu_rkv, mu_wa, w0, w_up, a0, a_up, k_k, k_a, r_k):
    n, d3 = rkv.shape
    d_b = d3 // 3
    lora = wa.shape[1] // 2
    tb = _row_tile(seq, 256)
    nb = seq // tb
    hb = tb // 8
    last8 = n // 8 - 1

    pos = np.arange(tb)
    same = (pos[:, None] // SCAN_CHUNK) == (pos[None, :] // SCAN_CHUNK)
    tri = np.stack([np.concatenate([same & (pos[None, :] <= pos[:, None]), same]),
                    np.concatenate([same & (pos[None, :] >= pos[:, None]), same])])
    tri = jnp.asarray(tri, BF16)
    ch = np.arange(d_b) // RWKV_HEAD
    gsum = jnp.asarray(ch[:, None] == ch[None, :], BF16)
    wup = jnp.concatenate([w_up, jnp.zeros((2, lora - w_up.shape[1], d_b), F32)], axis=1).astype(BF16)
    aup = jnp.concatenate([jnp.zeros((2, lora - a_up.shape[1], d_b), F32), a_up], axis=1).astype(BF16)

    fwd = lambda b, i: (b * nb + i, 0)
    bwd = lambda b, i: (b * nb + nb - 1 - i, 0)
    fwd_halo = lambda b, i: (jnp.maximum((b * nb + i) * hb - 1, 0), 0)
    bwd_halo = lambda b, i: (jnp.minimum((b * nb + nb - i) * hb, last8), 0)
    const2 = lambda b, i: (0, 0)
    const3 = lambda b, i: (0, 0, 0)
    out_spec_f = pl.BlockSpec((tb, d_b), fwd)
    out_spec_b = pl.BlockSpec((tb, d_b), bwd)
    out_sds = jax.ShapeDtypeStruct((n, d_b), F32)
    return pl.pallas_call(
        functools.partial(_rwkv_scan_body, tb=tb, d_b=d_b),
        grid=(batch, nb),
        in_specs=[
            pl.BlockSpec((tb, d3), fwd), pl.BlockSpec((8, d3), fwd_halo),
            pl.BlockSpec((tb, lora), fwd), pl.BlockSpec((8, lora), fwd_halo),
            pl.BlockSpec((tb, d3), bwd), pl.BlockSpec((8, d3), bwd_halo),
            pl.BlockSpec((tb, lora), lambda b, i: (b * nb + nb - 1 - i, 1)),
            pl.BlockSpec((8, lora), lambda b, i: (jnp.minimum((b * nb + nb - i) * hb, last8), 1)),
            pl.BlockSpec((2, 2 * tb, tb), const3),
            pl.BlockSpec((d_b, d_b), const2),
            pl.BlockSpec((2, d3), const2),
            pl.BlockSpec((2, lora), const2),
            pl.BlockSpec((2, d_b), const2),
            pl.BlockSpec((2, lora, d_b), const3),
            pl.BlockSpec((2, d_b), const2),
            pl.BlockSpec((2, lora, d_b), const3),
            pl.BlockSpec((1, d_b), const2),
            pl.BlockSpec((1, d_b), const2),
            pl.BlockSpec((1, d_b), const2),
        ],
        out_specs=[out_spec_f, out_spec_f, out_spec_b, out_spec_b],
        out_shape=[out_sds] * 4,
        scratch_shapes=[pltpu.VMEM((2, d_b // LANES, RWKV_HEAD, LANES), F32)]
        + [pltpu.VMEM((2, tb, d_b), F32)] * 8,
        compiler_params=_params("parallel", "arbitrary"),
    )(rkv, rkv, wa, wa, rkv, rkv, wa, wa, tri, gsum, mu_rkv, mu_wa, w0, wup, a0, aup,
      k_k.reshape(1, d_b), k_a.reshape(1, d_b), r_k.reshape(1, d_b))


def _rwkv_out_body(x_ref, ya_ref, yf_ref, yb_ref, bonf_ref, bonb_ref, gate_ref, gmean_ref,
                   gnw_ref, gnb_ref, woa_ref, wob_ref, o_ref):
    gmean = gmean_ref[...]
    y = yf_ref[...] + yb_ref[...]
    dev = y - _dot_exact_rhs(y, gmean)
    var = _dot_exact_rhs(dev * dev, gmean)
    y = dev * lax.rsqrt(var + RWKV_GN_EPS) * gnw_ref[...] + gnb_ref[...]
    y = (y + (bonf_ref[...] + bonb_ref[...])) * _sigmoid(gate_ref[...])
    o_ref[...] = (x_ref[...] + jnp.dot(ya_ref[...], woa_ref[...], preferred_element_type=F32)
                  + jnp.dot(y.astype(BF16), wob_ref[...], preferred_element_type=F32))


def _rwkv_out(x, ya, yf, yb, bonf, bonb, gate, gn_w, gn_b, w_out):
    n, d = x.shape
    d_a, d_b = ya.shape[1], yf.shape[1]
    tm = _row_tile(n, 512)
    ch = np.arange(d_b) // RWKV_HEAD
    gmean = jnp.asarray((ch[:, None] == ch[None, :]) / RWKV_HEAD, BF16)
    w_out = w_out.astype(BF16)
    tok = lambda w: pl.BlockSpec((tm, w), lambda i: (i, 0))
    const = lambda r, c: pl.BlockSpec((r, c), lambda i: (0, 0))
    return pl.pallas_call(
        _rwkv_out_body,
        grid=(n // tm,),
        in_specs=[tok(d), tok(d_a), tok(d_b), tok(d_b), tok(d_b), tok(d_b), tok(d_b),
                  const(d_b, d_b), const(1, d_b), const(1, d_b), const(d_a, d), const(d_b, d)],
        out_specs=tok(d),
        out_shape=jax.ShapeDtypeStruct((n, d), F32),
        compiler_params=_params("parallel"),
    )(x, ya, yf, yb, bonf, bonb, gate, gmean, gn_w.reshape(1, d_b), gn_b.reshape(1, d_b),
      w_out[:d_a], w_out[d_a:])


def _attn_proj_body(x_ref, g_ref, w_ref, hmean_ref, qn_ref, kn_ref, q_ref, k_ref, v_ref, *, d_attn):
    h = _rms(x_ref[...], g_ref[...]).astype(BF16)
    hmean = hmean_ref[...]

    def head_rms(t, gain):
        ms = jnp.concatenate(
            [_dot_exact_rhs(t[:, c:c + LANES] * t[:, c:c + LANES], hmean) for c in range(0, d_attn, LANES)],
            axis=1)
        return t * lax.rsqrt(ms + NORM_EPS) * gain

    q = jnp.dot(h, w_ref[:, 0:d_attn], preferred_element_type=F32)
    q_ref[...] = head_rms(q, qn_ref[...]) * (ATTN_HEAD_DIM ** -0.5 * LOG2_E)
    k = jnp.dot(h, w_ref[:, d_attn:2 * d_attn], preferred_element_type=F32)
    k_ref[...] = head_rms(k, kn_ref[...])
    v_ref[...] = jnp.dot(h, w_ref[:, 2 * d_attn:], preferred_element_type=F32)


def _attn_proj(x, g, w_in, q_norm, k_norm):
    n, d = x.shape
    d_attn = w_in.shape[1] // 3
    heads = d_attn // ATTN_HEAD_DIM
    tm = _row_tile(n, 512)
    ch = np.arange(LANES) // ATTN_HEAD_DIM
    hmean = jnp.asarray((ch[:, None] == ch[None, :]) / ATTN_HEAD_DIM, BF16)
    tok = pl.BlockSpec((tm, d_attn), lambda i: (i, 0))
    const = lambda r, c: pl.BlockSpec((r, c), lambda i: (0, 0))
    sds = jax.ShapeDtypeStruct((n, d_attn), F32)
    return pl.pallas_call(
        functools.partial(_attn_proj_body, d_attn=d_attn),
        grid=(n // tm,),
        in_specs=[pl.BlockSpec((tm, d), lambda i: (i, 0)), const(1, d), const(d, 3 * d_attn),
                  const(LANES, LANES), const(1, d_attn), const(1, d_attn)],
        out_specs=[tok, tok, tok],
        out_shape=[sds, sds, sds],
        compiler_params=_params("parallel"),
    )(x, g.reshape(1, d), w_in.astype(BF16), hmean,
      jnp.tile(q_norm, heads).reshape(1, d_attn), jnp.tile(k_norm, heads).reshape(1, d_attn))


ATTN_TILE = ATTN_RADIUS * max(DILATIONS)
ATTN_BATCH = 8
LOG2_E = math.log2(math.e)


def _dilated_attn_body(slopes_ref, q_ref, kp_ref, kc_ref, kn_ref, vp_ref, vc_ref, vn_ref, o_ref,
                       m_s, l_s, acc_s, *, nt):
    it, pair = pl.program_id(1), pl.program_id(2)
    rad = ATTN_RADIUS
    run_blocks = functools.partial(_attn_blocks, q_ref, kp_ref, kc_ref, kn_ref, vp_ref, vc_ref, vn_ref,
                                   m_s, l_s, acc_s)
    qa = lax.broadcasted_iota(jnp.int32, (rad, 3 * rad), 0)
    kj = lax.broadcasted_iota(jnp.int32, (rad, 3 * rad), 1) - rad
    dist = jnp.abs(qa - kj)
    band = dist <= rad
    ok_prev = band & (kj >= jnp.where(it > 0, -rad, 0))
    ok_next = band & (kj < jnp.where(it < nt - 1, 2 * rad, rad))
    ok_both = ok_prev & ok_next

    for d in DILATIONS:
        nsub = ATTN_TILE // (rad * d)
        dist_f = (dist * d).astype(F32)
        biases = []
        for hh in range(LANES // ATTN_HEAD_DIM):
            base = (-slopes_ref[pair * (LANES // ATTN_HEAD_DIM) + hh] * dist_f) * LOG2_E
            biases.append({(False, False): jnp.where(band, base, NEG_INF),
                           (True, False): jnp.where(ok_prev, base, NEG_INF),
                           (False, True): jnp.where(ok_next, base, NEG_INF),
                           (True, True): jnp.where(ok_both, base, NEG_INF)})
        specs = [(r, i) for r in range(d) for i in range(nsub)]
        for b0 in range(0, len(specs), ATTN_BATCH):
            run_blocks(d, nsub, specs[b0:b0 + ATTN_BATCH], biases)
    o_ref[...] = (acc_s[...] / l_s[...]).astype(BF16)


def _attn_blocks(q_ref, kp_ref, kc_ref, kn_ref, vp_ref, vc_ref, vn_ref, m_s, l_s, acc_s, d, nsub, specs, biases):
    rad = ATTN_RADIUS
    lo = lax.broadcasted_iota(jnp.int32, (1, LANES), 1) < ATTN_HEAD_DIM

    def strided(start, d):
        return pl.ds(start, rad, stride=d) if d > 1 else pl.ds(start, rad)

    blocks = []
    for r, i in specs:
        kps, vps = [], []
        for piece in (-1, 0, 1):
            off = rad * d * (i + piece) + r
            if off < 0:
                kr, vr, off = kp_ref, vp_ref, off + ATTN_TILE
            elif off >= ATTN_TILE:
                kr, vr, off = kn_ref, vn_ref, off - ATTN_TILE
            else:
                kr, vr = kc_ref, vc_ref
            kps.append(kr[strided(off, d), :])
            vps.append(vr[strided(off, d), :])
        rows = strided(r + rad * d * i, d)
        blocks.append(dict(rows=rows, edge=(i == 0, i == nsub - 1), q=q_ref[rows, :],
                           kw=jnp.concatenate(kps, axis=0).astype(BF16),
                           vw=jnp.concatenate(vps, axis=0).astype(BF16)))
    for blk in blocks:
        blk["s"] = [_dot_nt(jnp.where(~lo if hh else lo, blk["q"], 0.0), blk["kw"]) + biases[hh][blk["edge"]]
                    for hh in range(LANES // ATTN_HEAD_DIM)]
    for blk in blocks:
        blk["m"] = [jnp.max(s, axis=-1, keepdims=True) for s in blk["s"]]
    for blk in blocks:
        blk["p"] = [jnp.exp2(s - m) for s, m in zip(blk["s"], blk["m"])]
    for blk in blocks:
        blk["l"] = [jnp.sum(p, axis=-1, keepdims=True) for p in blk["p"]]
        blk["acc"] = [_dot(p, blk["vw"]) for p in blk["p"]]
    for blk in blocks:
        rows = blk["rows"]
        m_new = jnp.where(lo, blk["m"][0], blk["m"][1])
        l_new = jnp.where(lo, blk["l"][0], blk["l"][1])
        acc_new = jnp.where(lo, blk["acc"][0], blk["acc"][1])
        if d == DILATIONS[0]:
            m_s[rows, :], l_s[rows, :], acc_s[rows, :] = m_new, l_new, acc_new
        else:
            m_old = m_s[rows, :]
            m_tot = jnp.maximum(m_old, m_new)
            w_old, w_new = jnp.exp2(m_old - m_tot), jnp.exp2(m_new - m_tot)
            m_s[rows, :] = m_tot
            l_s[rows, :] = w_old * l_s[rows, :] + w_new * l_new
            acc_s[rows, :] = w_old * acc_s[rows, :] + w_new * acc_new


def _dilated_attn(q, k, v, slopes, batch, seq):
    n, d_attn = q.shape
    assert seq % ATTN_TILE == 0
    nt = seq // ATTN_TILE
    last = n // ATTN_TILE - 1
    cur = pl.BlockSpec((ATTN_TILE, LANES), lambda b, i, p: (b * nt + i, p))
    prev = pl.BlockSpec((ATTN_TILE, LANES), lambda b, i, p: (jnp.maximum(b * nt + i - 1, 0), p))
    nxt = pl.BlockSpec((ATTN_TILE, LANES), lambda b, i, p: (jnp.minimum(b * nt + i + 1, last), p))
    return pl.pallas_call(
        functools.partial(_dilated_attn_body, nt=nt),
        grid=(batch, nt, d_attn // LANES),
        in_specs=[pl.BlockSpec(memory_space=pltpu.SMEM), cur, prev, cur, nxt, prev, cur, nxt],
        out_specs=cur,
        out_shape=jax.ShapeDtypeStruct((n, d_attn), BF16),
        scratch_shapes=[pltpu.VMEM((ATTN_TILE, LANES), F32)] * 3,
        compiler_params=_params("parallel", "parallel", "arbitrary"),
    )(slopes, q, k, k, k, v, v, v)


def _proj_residual_body(x_ref, a_ref, w_ref, o_ref):
    o_ref[...] = x_ref[...] + jnp.dot(a_ref[...], w_ref[...], preferred_element_type=F32)


def _proj_residual(x, a, w):
    n, d = x.shape
    d_in = a.shape[1]
    tm = _row_tile(n, 512)
    return pl.pallas_call(
        _proj_residual_body,
        grid=(n // tm,),
        in_specs=[pl.BlockSpec((tm, d), lambda i: (i, 0)), pl.BlockSpec((tm, d_in), lambda i: (i, 0)),
                  pl.BlockSpec((d_in, d), lambda i: (0, 0))],
        out_specs=pl.BlockSpec((tm, d), lambda i: (i, 0)),
        out_shape=jax.ShapeDtypeStruct((n, d), F32),
        compiler_params=_params("parallel"),
    )(x, a, w.astype(BF16))


def _dilated_attention_fused(x, batch, seq, g, w_in, w_out, q_norm, k_norm):
    q, k, v = _attn_proj(x, g, w_in, q_norm, k_norm)
    heads = q.shape[1] // ATTN_HEAD_DIM
    slopes = 2.0 ** (-8.0 * jnp.arange(1, heads + 1, dtype=F32) / heads)
    return _proj_residual(x, _dilated_attn(q, k, v, slopes, batch, seq), w_out)


def _sgu_rwkv_mixer(x, batch, seq, g, w_in, w_out, sgu_norm, sgu_w, sgu_b, mu_rkv, mu_wa, w0, w_up,
                    a0, a_up, k_k, k_a, r_k, gn_w, gn_b):
    d_a = sgu_norm.shape[0]
    d_b = k_k.shape[0]
    ya, rkv, gate, wa = _proj_sgu(x, g, w_in, sgu_norm, sgu_w, sgu_b, d_a=d_a, d_b=d_b)
    yf, bonf, yb, bonb = _rwkv_scan(rkv, wa, batch, seq, mu_rkv, mu_wa, w0, w_up, a0, a_up,
                                    k_k, k_a, r_k.reshape(-1))
    return _rwkv_out(x, ya, yf, yb, bonf, bonb, gate, gn_w, gn_b, w_out)


def kernel(x_prompt, x_sample, ffn1_norm, ffn1_w_in, ffn1_w_out, mix_norm, ffn2_norm, ffn2_w_in,
           ffn2_w_out, block_norm, ab_w_in, ab_w_out, sgu_norm, sgu_w, sgu_b, rwkv_mu_rkv, rwkv_mu_wa,
           rwkv_w0, rwkv_w_up, rwkv_a0, rwkv_a_up, rwkv_k_k, rwkv_k_a, rwkv_r_k, rwkv_gn_w, rwkv_gn_b,
           attn_w_in, attn_w_out, attn_q_norm, attn_k_norm):
    depth = ffn1_norm.shape[0]

    def trunk(x3):
        batch, seq, d = x3.shape
        x = x3.reshape(batch * seq, d)
        for i in range(depth):
            j = i // 2
            x = _ffn(x, ffn1_norm[i], ffn1_w_in[i], ffn1_w_out[i], ffn1_norm[i], final_norm=False)
            if i % 2 == 0:
                x = _sgu_rwkv_mixer(x, batch, seq, mix_norm[i], ab_w_in[j], ab_w_out[j], sgu_norm[j],
                                    sgu_w[j], sgu_b[j], rwkv_mu_rkv[j], rwkv_mu_wa[j], rwkv_w0[j],
                                    rwkv_w_up[j], rwkv_a0[j], rwkv_a_up[j], rwkv_k_k[j], rwkv_k_a[j],
                                    rwkv_r_k[j], rwkv_gn_w[j], rwkv_gn_b[j])
            else:
                x = _dilated_attention_fused(x, batch, seq, mix_norm[i], attn_w_in[j], attn_w_out[j],
                                       attn_q_norm[j], attn_k_norm[j])
            x = _ffn(x, ffn2_norm[i], ffn2_w_in[i], ffn2_w_out[i], block_norm[i], final_norm=True)
        return x.reshape(batch, seq, d)

    return trunk(x_prompt), trunk(x_sample)
```

```python
import functools
import math

import numpy as np
import jax
import jax.numpy as jnp
from jax import lax
from jax.experimental import pallas as pl
from jax.experimental.pallas import tpu as pltpu

F32 = jnp.float32
BF16 = jnp.bfloat16

NORM_EPS = 1e-6
SGU_GROUPS = 4
SGU_CHUNK = 128
RWKV_HEAD = 64
RWKV_LORA_W = 64
RWKV_GN_EPS = 64e-5
ATTN_HEAD_DIM = 64
ATTN_RADIUS = 64
DILATIONS = (1, 4, 16)
NEG_INF = -1e30
LOG2_E = math.log2(math.e)

LANES = 128
MXU_TILE = 256
SCAN_CHUNK = 64
VMEM_LIMIT_BYTES = 52 * 1024 * 1024


def _params(*semantics):
    return pltpu.CompilerParams(dimension_semantics=semantics, vmem_limit_bytes=VMEM_LIMIT_BYTES)


def _dot(a, b):
    return jnp.dot(a.astype(BF16), b.astype(BF16), preferred_element_type=F32)


def _dot_nt(a, b):
    return lax.dot_general(a.astype(BF16), b.astype(BF16), (((1,), (1,)), ((), ())),
                           preferred_element_type=F32)


def _split(x):
    hi = x.astype(BF16)
    lo = (x - hi.astype(F32)).astype(BF16)
    return hi, lo


def _dot_exact_lhs(lhs_bf16, x):
    hi, lo = _split(x)
    return _dot(lhs_bf16, hi) + _dot(lhs_bf16, lo)


def _sigmoid(x):
    return 0.5 * jnp.tanh(0.5 * x) + 0.5


def _dot_exact_rhs(x, rhs_bf16):
    hi, lo = _split(x)
    return _dot(hi, rhs_bf16) + _dot(lo, rhs_bf16)


def _rms(x, g, eps=NORM_EPS):
    return x * lax.rsqrt(jnp.mean(x * x, axis=-1, keepdims=True) + eps) * g


def _row_tile(n, want):
    t = min(n, want)
    assert n % t == 0, (n, t)
    return t


def _ffn_body(*refs, n_pre, final_norm, f_splits):
    x_ref, g_ref, wg_ref, wu_ref, wo_ref = refs[:5]
    o_ref = refs[-1]
    x = x_ref[...]
    for j in range(n_pre):
        a_ref, wpre_ref = refs[5 + 2 * j:7 + 2 * j]
        x = x + jnp.dot(a_ref[...], wpre_ref[...], preferred_element_type=F32)
    h = _rms(x, g_ref[...]).astype(BF16)
    y = x
    for lo, hi in f_splits:
        gate = jnp.dot(h, wg_ref[:, lo:hi], preferred_element_type=F32)
        up = jnp.dot(h, wu_ref[:, lo:hi], preferred_element_type=F32)
        act = (gate * _sigmoid(gate) * up).astype(BF16)
        y = y + 0.5 * jnp.dot(act, wo_ref[lo:hi, :], preferred_element_type=F32)
    if final_norm:
        y = _rms(y, refs[-2][...])
    o_ref[...] = y


def _ffn(x, g, w_in, w_out, g2=None, pre=()):
    n, d = x.shape
    f_dim = w_out.shape[0]
    tm = _row_tile(n, 512)
    final_norm = g2 is not None
    resident = lambda shape, imap: pl.BlockSpec(shape, imap, pipeline_mode=pl.Buffered(1))
    extra_specs, extra_args = [], []
    for a, w_pre in pre:
        extra_specs += [pl.BlockSpec((tm, a.shape[1]), lambda i: (i, 0)), resident(w_pre.shape, lambda i: (0, 0))]
        extra_args += [a, w_pre.astype(BF16)]
    if final_norm:
        extra_specs.append(pl.BlockSpec((1, d), lambda i: (0, 0)))
        extra_args.append(g2.reshape(1, d))
    assert f_dim % MXU_TILE == 0
    mid = (f_dim // MXU_TILE // 2) * MXU_TILE
    f_splits = ((0, mid), (mid, f_dim))
    w_in = w_in.astype(BF16)
    w_out = w_out.astype(BF16)
    return pl.pallas_call(
        functools.partial(_ffn_body, n_pre=len(pre), final_norm=final_norm, f_splits=f_splits),
        grid=(n // tm,),
        in_specs=[
            pl.BlockSpec((tm, d), lambda i: (i, 0)),
            pl.BlockSpec((1, d), lambda i: (0, 0)),
            resident((d, f_dim), lambda i: (0, 0)),
            resident((d, f_dim), lambda i: (0, 1)),
            resident((f_dim, d), lambda i: (0, 0)),
        ] + extra_specs,
        out_specs=pl.BlockSpec((tm, d), lambda i: (i, 0)),
        out_shape=jax.ShapeDtypeStruct((n, d), F32),
        compiler_params=_params("parallel"),
    )(x, g.reshape(1, d), w_in, w_in, w_out, *extra_args)


def _proj_sgu_body(x_ref, g_ref, w_ref, sgn_ref, sw_ref, sb_ref, ya_ref, rkv_ref, gate_ref, wa_ref,
                   *, d_a, d_b):
    h = _rms(x_ref[...], g_ref[...]).astype(BF16)
    o_rkv, o_g, o_wa = 2 * d_a, 2 * d_a + 3 * d_b, 2 * d_a + 4 * d_b
    u = jnp.dot(h, w_ref[:, 0:d_a], preferred_element_type=F32)
    v = jnp.dot(h, w_ref[:, d_a:2 * d_a], preferred_element_type=F32)
    rkv_ref[...] = jnp.dot(h, w_ref[:, o_rkv:o_g], preferred_element_type=F32)
    gate_ref[...] = jnp.dot(h, w_ref[:, o_g:o_wa], preferred_element_type=F32).astype(BF16)
    wa_ref[...] = jnp.dot(h, w_ref[:, o_wa:], preferred_element_type=F32)
    u = jax.nn.gelu(u)
    v = _rms(jax.nn.gelu(v), sgn_ref[...]).astype(BF16)
    gw = d_a // SGU_GROUPS
    for c in range(x_ref.shape[0] // SGU_CHUNK):
        rows = slice(c * SGU_CHUNK, (c + 1) * SGU_CHUNK)
        for grp in range(SGU_GROUPS):
            cols = slice(grp * gw, (grp + 1) * gw)
            mixed = jnp.dot(sw_ref[grp], v[rows, cols], preferred_element_type=F32) + sb_ref[:, cols]
            ya_ref[rows, cols] = (u[rows, cols] * mixed).astype(BF16)


def _proj_sgu(x, g, w_in, sgu_norm, sgu_w, sgu_b, *, d_a, d_b):
    n, d = x.shape
    d_in = w_in.shape[1]
    d_wa = d_in - 2 * d_a - 4 * d_b
    tm = _row_tile(n, 512)
    gw = d_a // SGU_GROUPS
    bias = jnp.repeat(sgu_b.T, gw, axis=1)
    return pl.pallas_call(
        functools.partial(_proj_sgu_body, d_a=d_a, d_b=d_b),
        grid=(n // tm,),
        in_specs=[
            pl.BlockSpec((tm, d), lambda i: (i, 0)),
            pl.BlockSpec((1, d), lambda i: (0, 0)),
            pl.BlockSpec((d, d_in), lambda i: (0, 0)),
            pl.BlockSpec((1, d_a), lambda i: (0, 0)),
            pl.BlockSpec((SGU_GROUPS, SGU_CHUNK, SGU_CHUNK), lambda i: (0, 0, 0)),
            pl.BlockSpec((SGU_CHUNK, d_a), lambda i: (0, 0)),
        ],
        out_specs=[
            pl.BlockSpec((tm, d_a), lambda i: (i, 0)),
            pl.BlockSpec((tm, 3 * d_b), lambda i: (i, 0)),
            pl.BlockSpec((tm, d_b), lambda i: (i, 0)),
            pl.BlockSpec((tm, d_wa), lambda i: (i, 0)),
        ],
        out_shape=[
            jax.ShapeDtypeStruct((n, d_a), BF16),
            jax.ShapeDtypeStruct((n, 3 * d_b), F32),
            jax.ShapeDtypeStruct((n, d_b), BF16),
            jax.ShapeDtypeStruct((n, d_wa), F32),
        ],
        compiler_params=_params("parallel"),
    )(x, g.reshape(1, d), w_in.astype(BF16), sgu_norm.reshape(1, d_a), sgu_w.astype(BF16), bias)


def _rwkv_scan_body(rkvf_ref, hrf_ref, waf_ref, hwf_ref, rkvb_ref, hrb_ref, wab_ref, hwb_ref,
                    tri_ref, gsum_ref, mu_rkv_ref, mu_wa_ref, w0_ref, wup_ref, a0_ref, aup_ref,
                    kk_ref, ka_ref, rk_ref,
                    yf_ref, bonf_ref, yb_ref, bonb_ref,
                    s_ref, a_s, k_s, b_s, r_s, v_s, ae_s, ke_s, tot_s, *, tb, d_b):
    n = pl.program_id(1)
    nc = tb // SCAN_CHUNK
    assert RWKV_HEAD == SCAN_CHUNK and 2 * RWKV_HEAD == LANES
    first = n == 0
    row = lax.broadcasted_iota(jnp.int32, (tb, 1), 0)
    gsum = gsum_ref[...]

    @pl.when(first)
    def _():
        s_ref[...] = jnp.zeros_like(s_ref)

    def group_sum(x):
        return _dot(x, gsum)

    def prep(d, x, halo, wa, wa_halo, bon_ref):
        halo = jnp.where(first, 0.0, halo)
        wa_halo = jnp.where(first, 0.0, wa_halo)
        if d == 0:
            xs = jnp.where(row == 0, halo, pltpu.roll(x, 1, 0))
            was = jnp.where(row == 0, wa_halo, pltpu.roll(wa, 1, 0))
        else:
            xs = jnp.where(row == tb - 1, halo, pltpu.roll(x, tb - 1, 0))
            was = jnp.where(row == tb - 1, wa_halo, pltpu.roll(wa, tb - 1, 0))
        z = x + (xs - x) * mu_rkv_ref[d:d + 1, :]
        zwa = wa + (was - wa) * mu_wa_ref[d:d + 1, :]
        r, k, v = z[:, 0:d_b], z[:, d_b:2 * d_b], z[:, 2 * d_b:3 * d_b]
        w_raw = w0_ref[d:d + 1, :] + _dot(jnp.tanh(zwa), wup_ref[d])
        lw = -_sigmoid(w_raw) * (math.exp(-0.5) * LOG2_E)
        a = _sigmoid(a0_ref[d:d + 1, :] + _dot(zwa, aup_ref[d]))
        kk = k * kk_ref[...]
        kk = kk * jnp.minimum(lax.rsqrt(group_sum(kk * kk)), 1e12)
        k2 = k * (1.0 + (a - 1.0) * ka_ref[...])
        bon_ref[...] = (group_sum(r * k2 * rk_ref[...]) * v).astype(BF16)
        cums = _dot_exact_lhs(tri_ref[d], lw)
        cum, tot = cums[:tb], cums[tb:]
        e_neg = jnp.exp2(-cum)
        e_end = jnp.exp2(tot - cum)
        kka = kk * a
        a_s[d] = kka * e_neg
        k_s[d] = k2 * e_neg
        b_s[d] = kk * jnp.exp2(cum - lw)
        r_s[d] = r * jnp.exp2(cum)
        v_s[d] = v
        ae_s[d] = kka * e_end
        ke_s[d] = k2 * e_end
        tot_s[d] = tot

    prep(0, rkvf_ref[...], hrf_ref[7:8, :], waf_ref[...], hwf_ref[7:8, :], bonf_ref)
    prep(1, rkvb_ref[...], hrb_ref[0:1, :], wab_ref[...], hwb_ref[0:1, :], bonb_ref)

    y_refs = (yf_ref, yb_ref)
    cc_ = SCAN_CHUNK
    lane = lax.broadcasted_iota(jnp.int32, (1, LANES), 1)
    lo = lane < cc_
    row = lax.broadcasted_iota(jnp.int32, (cc_, LANES), 0)
    col = lax.broadcasted_iota(jnp.int32, (cc_, LANES), 1)
    tok = col & (cc_ - 1)
    diag_lo, diag_hi = col == row, col == row + cc_
    lane2 = lax.broadcasted_iota(jnp.int32, (1, 2 * LANES), 1)
    keep = (lane2 >= cc_, (lane2 < cc_) | (lane2 >= LANES))
    zeros1 = jnp.zeros((cc_, LANES), F32)
    zeros2 = jnp.zeros((cc_, 2 * LANES), F32)
    rows_cat = lambda *xs: jnp.concatenate(xs, axis=0)
    lanes_cat = lambda *xs: jnp.concatenate(xs, axis=1)

    def chunk_step(c, carry):
        pairs = []
        for d in range(2):
            r0 = pl.multiple_of((c if d == 0 else nc - 1 - c) * cc_, cc_)
            rows = pl.ds(r0, cc_)
            strict = (tok < row) if d == 0 else (tok > row)
            incl = (tok <= row) if d == 0 else (tok >= row)
            for p in range(d_b // LANES):
                pairs.append((d, p, rows, slice(p * LANES, (p + 1) * LANES), strict, incl))

        chains = []
        for d, p, rows, cols, strict, incl in pairs:
            bp, rp, vp = b_s[d, rows, cols], r_s[d, rows, cols], v_s[d, rows, cols]
            ap, kp = a_s[d, rows, cols], k_s[d, rows, cols]
            bx, rx = pltpu.roll(bp, cc_, 1), pltpu.roll(rp, cc_, 1)
            zv = lanes_cat(zeros1, vp)
            t = rows_cat(ae_s[d, rows, cols], ke_s[d, rows, cols]).T
            px = pltpu.roll(jnp.exp2(tot_s[d, rows, cols]), cc_, 1)
            for odd in (False, True):
                akt = (rows_cat(kp, ap) if odd else rows_cat(ap, kp)).T
                chains.append(dict(d=d, p=p, odd=odd, mine=~lo if odd else lo, strict=strict, incl=incl,
                                   bp=bp, rp=rp, vp=vp, akt=akt, bx=bx, rx=rx, zv=zv, px=px,
                                   t=t[cc_:] if odd else t[:cc_]))

        for ch in chains:
            mine = ch["mine"]
            ch["g"] = _dot(rows_cat(jnp.where(mine, ch["bp"], 0.0), jnp.where(mine, ch["rp"], 0.0)), ch["akt"])
        for ch in chains:
            ch["gt"] = jnp.where(ch["strict"], ch["g"][:cc_], 0.0)
            ch["w"] = _dot(jnp.where(ch["mine"], 0.0, ch["gt"]),
                           rows_cat(ch["vp"], zeros1) if ch["odd"] else rows_cat(zeros1, ch["vp"]))
        for ch in chains:
            ch["s"] = lanes_cat(jnp.where(ch["mine"], -ch["gt"], ch["bx"]), ch["w"])

        for _ in range(6):
            for ch in chains:
                s = ch["s"]
                rhs = rows_cat(zeros2, s) if ch["odd"] else rows_cat(s, zeros2)
                ch["s"] = _dot(s[:, :LANES], rhs) + jnp.where(keep[ch["odd"]], s, 0.0)

        for ch in chains:
            s, zv = ch["s"], ch["zv"]
            ch["ry"] = _dot(jnp.where(ch["incl"], ch["g"][cc_:], 0.0),
                            rows_cat(zv, -s) if ch["odd"] else rows_cat(-s, zv))
        for ch in chains:
            ch["pp"] = _dot(ch["t"], rows_cat(ch["s"], -ch["zv"]))
        outs = {}
        for ch in chains:
            d, p, odd, ry, pp = ch["d"], ch["p"], ch["odd"], ch["ry"], ch["pp"]
            phi = jnp.where(diag_lo if odd else diag_hi, ch["px"], 0.0) - pp[:, :LANES]
            lhs = rows_cat(ch["rx"] + ry[:, :LANES], phi)
            h0 = s_ref[d, p]
            out = _dot(lhs, rows_cat(h0, zeros1) if odd else rows_cat(zeros1, h0))
            outs[(d, p, odd)] = out + rows_cat(ry[:, LANES:], -pp[:, LANES:])
        for d, p, rows, cols, strict, incl in pairs:
            res = jnp.where(lo, outs[(d, p, False)], outs[(d, p, True)])
            y_refs[d][rows, cols] = res[:cc_].astype(BF16)
            s_ref[d, p] = res[cc_:]
        return carry

    lax.fori_loop(0, nc, chunk_step, 0)


def _rwkv_scan(rkv, wa, batch, seq, mu_rkv, mu_wa, w0, w_up, a0, a_up, k_k, k_a, r_k):
    n, d3 = rkv.shape
    d_b = d3 // 3
    lora = wa.shape[1] // 2
    tb = _row_tile(seq, 256)
    nb = seq // tb
    hb = tb // 8
    last8 = n // 8 - 1

    pos = np.arange(tb)
    same = (pos[:, None] // SCAN_CHUNK) == (pos[None, :] // SCAN_CHUNK)
    tri = np.stack([np.concatenate([same & (pos[None, :] <= pos[:, None]), same]),
                    np.concatenate([same & (pos[None, :] >= pos[:, None]), same])])
    tri = jnp.asarray(tri, BF16)
    ch = np.arange(d_b) // RWKV_HEAD
    gsum = jnp.asarray(ch[:, None] == ch[None, :], BF16)
    wup = jnp.concatenate([w_up, jnp.zeros((2, lora - w_up.shape[1], d_b), F32)], axis=1).astype(BF16)
    aup = jnp.concatenate([jnp.zeros((2, lora - a_up.shape[1], d_b), F32), a_up], axis=1).astype(BF16)

    fwd = lambda b, i: (b * nb + i, 0)
    bwd = lambda b, i: (b * nb + nb - 1 - i, 0)
    fwd_halo = lambda b, i: (jnp.maximum((b * nb + i) * hb - 1, 0), 0)
    bwd_halo = lambda b, i: (jnp.minimum((b * nb + nb - i) * hb, last8), 0)
    const2 = lambda b, i: (0, 0)
    const3 = lambda b, i: (0, 0, 0)
    out_spec_f = pl.BlockSpec((tb, d_b), fwd)
    out_spec_b = pl.BlockSpec((tb, d_b), bwd)
    out_sds = jax.ShapeDtypeStruct((n, d_b), BF16)
    return pl.pallas_call(
        functools.partial(_rwkv_scan_body, tb=tb, d_b=d_b),
        grid=(batch, nb),
        in_specs=[
            pl.BlockSpec((tb, d3), fwd), pl.BlockSpec((8, d3), fwd_halo),
            pl.BlockSpec((tb, lora), fwd), pl.BlockSpec((8, lora), fwd_halo),
            pl.BlockSpec((tb, d3), bwd), pl.BlockSpec((8, d3), bwd_halo),
            pl.BlockSpec((tb, lora), lambda b, i: (b * nb + nb - 1 - i, 1)),
            pl.BlockSpec((8, lora), lambda b, i: (jnp.minimum((b * nb + nb - i) * hb, last8), 1)),
            pl.BlockSpec((2, 2 * tb, tb), const3),
            pl.BlockSpec((d_b, d_b), const2),
            pl.BlockSpec((2, d3), const2),
            pl.BlockSpec((2, lora), const2),
            pl.BlockSpec((2, d_b), const2),
            pl.BlockSpec((2, lora, d_b), const3),
            pl.BlockSpec((2, d_b), const2),
            pl.BlockSpec((2, lora, d_b), const3),
            pl.BlockSpec((1, d_b), const2),
            pl.BlockSpec((1, d_b), const2),
            pl.BlockSpec((1, d_b), const2),
        ],
        out_specs=[out_spec_f, out_spec_f, out_spec_b, out_spec_b],
        out_shape=[out_sds] * 4,
        scratch_shapes=[pltpu.VMEM((2, d_b // LANES, RWKV_HEAD, LANES), F32)]
        + [pltpu.VMEM((2, tb, d_b), F32)] * 8,
        compiler_params=_params("parallel", "arbitrary"),
    )(rkv, rkv, wa, wa, rkv, rkv, wa, wa, tri, gsum, mu_rkv, mu_wa, w0, wup, a0, aup,
      k_k.reshape(1, d_b), k_a.reshape(1, d_b), r_k.reshape(1, d_b))


def _rwkv_post_body(yf_ref, yb_ref, bonf_ref, bonb_ref, gate_ref, gmean_ref, gnw_ref, gnb_ref, o_ref):
    gmean = gmean_ref[...]
    f32 = lambda ref: ref[...].astype(F32)
    y = f32(yf_ref) + f32(yb_ref)
    dev = y - _dot_exact_rhs(y, gmean)
    var = _dot(dev * dev, gmean)
    y = dev * lax.rsqrt(var + RWKV_GN_EPS) * gnw_ref[...] + gnb_ref[...]
    o_ref[...] = ((y + (f32(bonf_ref) + f32(bonb_ref))) * _sigmoid(f32(gate_ref))).astype(BF16)


def _rwkv_post(yf, yb, bonf, bonb, gate, gn_w, gn_b):
    n, d_b = yf.shape
    tm = _row_tile(n, 1024)
    ch = np.arange(d_b) // RWKV_HEAD
    gmean = jnp.asarray((ch[:, None] == ch[None, :]) / RWKV_HEAD, BF16)
    tok = pl.BlockSpec((tm, d_b), lambda i: (i, 0))
    const = lambda r, c: pl.BlockSpec((r, c), lambda i: (0, 0))
    return pl.pallas_call(
        _rwkv_post_body,
        grid=(n // tm,),
        in_specs=[tok] * 5 + [const(d_b, d_b), const(1, d_b), const(1, d_b)],
        out_specs=tok,
        out_shape=jax.ShapeDtypeStruct((n, d_b), BF16),
        compiler_params=_params("parallel"),
    )(yf, yb, bonf, bonb, gate, gmean, gn_w.reshape(1, d_b), gn_b.reshape(1, d_b))


def _attn_proj_body(x_ref, g_ref, w_ref, hmean_ref, qn_ref, kn_ref, q_ref, k_ref, v_ref, *, d_attn):
    h = _rms(x_ref[...], g_ref[...]).astype(BF16)
    hmean = hmean_ref[...]

    def head_rms(t, gain):
        ms = jnp.concatenate(
            [_dot(t[:, c:c + LANES] * t[:, c:c + LANES], hmean) for c in range(0, d_attn, LANES)],
            axis=1)
        return t * lax.rsqrt(ms + NORM_EPS) * gain

    q = jnp.dot(h, w_ref[:, 0:d_attn], preferred_element_type=F32)
    q_ref[...] = head_rms(q, qn_ref[...]) * (ATTN_HEAD_DIM ** -0.5 * LOG2_E)
    k = jnp.dot(h, w_ref[:, d_attn:2 * d_attn], preferred_element_type=F32)
    k_ref[...] = head_rms(k, kn_ref[...])
    v_ref[...] = jnp.dot(h, w_ref[:, 2 * d_attn:], preferred_element_type=F32)


def _attn_proj(x, g, w_in, q_norm, k_norm):
    n, d = x.shape
    d_attn = w_in.shape[1] // 3
    heads = d_attn // ATTN_HEAD_DIM
    tm = _row_tile(n, 512)
    ch = np.arange(LANES) // ATTN_HEAD_DIM
    hmean = jnp.asarray((ch[:, None] == ch[None, :]) / ATTN_HEAD_DIM, BF16)
    tok = pl.BlockSpec((tm, d_attn), lambda i: (i, 0))
    const = lambda r, c: pl.BlockSpec((r, c), lambda i: (0, 0))
    sds = jax.ShapeDtypeStruct((n, d_attn), F32)
    return pl.pallas_call(
        functools.partial(_attn_proj_body, d_attn=d_attn),
        grid=(n // tm,),
        in_specs=[pl.BlockSpec((tm, d), lambda i: (i, 0)), const(1, d), const(d, 3 * d_attn),
                  const(LANES, LANES), const(1, d_attn), const(1, d_attn)],
        out_specs=[tok, tok, tok],
        out_shape=[sds, sds, sds],
        compiler_params=_params("parallel"),
    )(x, g.reshape(1, d), w_in.astype(BF16), hmean,
      jnp.tile(q_norm, heads).reshape(1, d_attn), jnp.tile(k_norm, heads).reshape(1, d_attn))


ATTN_TILE = ATTN_RADIUS * max(DILATIONS)
ATTN_BATCH = 8


def _dilated_attn_body(slopes_ref, q_ref, kp_ref, kc_ref, kn_ref, vp_ref, vc_ref, vn_ref, o_ref,
                       m_s, l_s, acc_s, *, nt):
    it, pair = pl.program_id(1), pl.program_id(2)
    rad = ATTN_RADIUS
    run_blocks = functools.partial(_attn_blocks, q_ref, kp_ref, kc_ref, kn_ref, vp_ref, vc_ref, vn_ref,
                                   m_s, l_s, acc_s)
    qa = lax.broadcasted_iota(jnp.int32, (rad, 3 * rad), 0)
    kj = lax.broadcasted_iota(jnp.int32, (rad, 3 * rad), 1) - rad
    dist = jnp.abs(qa - kj)
    band = dist <= rad
    ok_prev = band & (kj >= jnp.where(it > 0, -rad, 0))
    ok_next = band & (kj < jnp.where(it < nt - 1, 2 * rad, rad))
    ok_both = ok_prev & ok_next

    for d in DILATIONS:
        nsub = ATTN_TILE // (rad * d)
        dist_f = (dist * d).astype(F32)
        biases = []
        for hh in range(LANES // ATTN_HEAD_DIM):
            base = (-slopes_ref[pair * (LANES // ATTN_HEAD_DIM) + hh] * dist_f) * LOG2_E
            biases.append({(False, False): jnp.where(band, base, NEG_INF),
                           (True, False): jnp.where(ok_prev, base, NEG_INF),
                           (False, True): jnp.where(ok_next, base, NEG_INF),
                           (True, True): jnp.where(ok_both, base, NEG_INF)})
        specs = [(r, i) for r in range(d) for i in range(nsub)]
        for b0 in range(0, len(specs), ATTN_BATCH):
            run_blocks(d, nsub, specs[b0:b0 + ATTN_BATCH], biases)
    o_ref[...] = (acc_s[...] / l_s[...]).astype(BF16)


def _attn_blocks(q_ref, kp_ref, kc_ref, kn_ref, vp_ref, vc_ref, vn_ref, m_s, l_s, acc_s, d, nsub, specs, biases):
    rad = ATTN_RADIUS
    lo = lax.broadcasted_iota(jnp.int32, (1, LANES), 1) < ATTN_HEAD_DIM

    def strided(start, d):
        return pl.ds(start, rad, stride=d) if d > 1 else pl.ds(start, rad)

    blocks, pieces = [], {}
    for r, i in specs:
        kps, vps = [], []
        for piece in (-1, 0, 1):
            off = rad * d * (i + piece) + r
            if off < 0:
                kr, vr, off = kp_ref, vp_ref, off + ATTN_TILE
            elif off >= ATTN_TILE:
                kr, vr, off = kn_ref, vn_ref, off - ATTN_TILE
            else:
                kr, vr = kc_ref, vc_ref
            key = (kr is kp_ref, kr is kn_ref, off)
            if key not in pieces:
                pieces[key] = (kr[strided(off, d), :].astype(BF16), vr[strided(off, d), :].astype(BF16))
            kps.append(pieces[key][0])
            vps.append(pieces[key][1])
        rows = strided(r + rad * d * i, d)
        blocks.append(dict(rows=rows, edge=(i == 0, i == nsub - 1), q=q_ref[rows, :],
                           kw=jnp.concatenate(kps, axis=0), vw=jnp.concatenate(vps, axis=0)))
    for blk in blocks:
        blk["s"] = [_dot_nt(jnp.where(~lo if hh else lo, blk["q"], 0.0), blk["kw"]) + biases[hh][blk["edge"]]
                    for hh in range(LANES // ATTN_HEAD_DIM)]
    for blk in blocks:
        blk["m"] = [jnp.max(s, axis=-1, keepdims=True) for s in blk["s"]]
    for blk in blocks:
        blk["p"] = [jnp.exp2(s - m) for s, m in zip(blk["s"], blk["m"])]
    for blk in blocks:
        blk["l"] = [jnp.sum(p, axis=-1, keepdims=True) for p in blk["p"]]
        blk["acc"] = [_dot(p, blk["vw"]) for p in blk["p"]]
    for blk in blocks:
        rows = blk["rows"]
        m_new = jnp.where(lo, blk["m"][0], blk["m"][1])
        l_new = jnp.where(lo, blk["l"][0], blk["l"][1])
        acc_new = jnp.where(lo, blk["acc"][0], blk["acc"][1])
        if d == DILATIONS[0]:
            m_s[rows, :], l_s[rows, :], acc_s[rows, :] = m_new, l_new, acc_new
        else:
            m_old = m_s[rows, :]
            m_tot = jnp.maximum(m_old, m_new)
            w_old, w_new = jnp.exp2(m_old - m_tot), jnp.exp2(m_new - m_tot)
            m_s[rows, :] = m_tot
            l_s[rows, :] = w_old * l_s[rows, :] + w_new * l_new
            acc_s[rows, :] = w_old * acc_s[rows, :] + w_new * acc_new


def _dilated_attn(q, k, v, slopes, batch, seq):
    n, d_attn = q.shape
    assert seq % ATTN_TILE == 0
    nt = seq // ATTN_TILE
    last = n // ATTN_TILE - 1
    cur = pl.BlockSpec((ATTN_TILE, LANES), lambda b, i, p: (b * nt + i, p))
    prev = pl.BlockSpec((ATTN_TILE, LANES), lambda b, i, p: (jnp.maximum(b * nt + i - 1, 0), p))
    nxt = pl.BlockSpec((ATTN_TILE, LANES), lambda b, i, p: (jnp.minimum(b * nt + i + 1, last), p))
    return pl.pallas_call(
        functools.partial(_dilated_attn_body, nt=nt),
        grid=(batch, nt, d_attn // LANES),
        in_specs=[pl.BlockSpec(memory_space=pltpu.SMEM), cur, prev, cur, nxt, prev, cur, nxt],
        out_specs=cur,
        out_shape=jax.ShapeDtypeStruct((n, d_attn), BF16),
        scratch_shapes=[pltpu.VMEM((ATTN_TILE, LANES), F32)] * 3,
        compiler_params=_params("parallel", "parallel", "arbitrary"),
    )(slopes, q, k, k, k, v, v, v)


def _dilated_attention(x, batch, seq, g, w_in, w_out, q_norm, k_norm):
    q, k, v = _attn_proj(x, g, w_in, q_norm, k_norm)
    heads = q.shape[1] // ATTN_HEAD_DIM
    slopes = 2.0 ** (-8.0 * jnp.arange(1, heads + 1, dtype=F32) / heads)
    return [(_dilated_attn(q, k, v, slopes, batch, seq), w_out)]


def _sgu_rwkv_mixer(x, batch, seq, g, w_in, w_out, sgu_norm, sgu_w, sgu_b, mu_rkv, mu_wa, w0, w_up,
                    a0, a_up, k_k, k_a, r_k, gn_w, gn_b):
    d_a = sgu_norm.shape[0]
    d_b = k_k.shape[0]
    ya, rkv, gate, wa = _proj_sgu(x, g, w_in, sgu_norm, sgu_w, sgu_b, d_a=d_a, d_b=d_b)
    yf, bonf, yb, bonb = _rwkv_scan(rkv, wa, batch, seq, mu_rkv, mu_wa, w0, w_up, a0, a_up,
                                    k_k, k_a, r_k.reshape(-1))
    y_rwkv = _rwkv_post(yf, yb, bonf, bonb, gate, gn_w, gn_b)
    return [(ya, w_out[:d_a]), (y_rwkv, w_out[d_a:])]


def kernel(x_prompt, x_sample, ffn1_norm, ffn1_w_in, ffn1_w_out, mix_norm, ffn2_norm, ffn2_w_in,
           ffn2_w_out, block_norm, ab_w_in, ab_w_out, sgu_norm, sgu_w, sgu_b, rwkv_mu_rkv, rwkv_mu_wa,
           rwkv_w0, rwkv_w_up, rwkv_a0, rwkv_a_up, rwkv_k_k, rwkv_k_a, rwkv_r_k, rwkv_gn_w, rwkv_gn_b,
           attn_w_in, attn_w_out, attn_q_norm, attn_k_norm):
    depth = ffn1_norm.shape[0]

    def trunk(x3):
        batch, seq, d = x3.shape
        x = x3.reshape(batch * seq, d)
        for i in range(depth):
            j = i // 2
            x = _ffn(x, ffn1_norm[i], ffn1_w_in[i], ffn1_w_out[i])
            if i % 2 == 0:
                mix = _sgu_rwkv_mixer(x, batch, seq, mix_norm[i], ab_w_in[j], ab_w_out[j], sgu_norm[j],
                                      sgu_w[j], sgu_b[j], rwkv_mu_rkv[j], rwkv_mu_wa[j], rwkv_w0[j],
                                      rwkv_w_up[j], rwkv_a0[j], rwkv_a_up[j], rwkv_k_k[j], rwkv_k_a[j],
                                      rwkv_r_k[j], rwkv_gn_w[j], rwkv_gn_b[j])
            else:
                mix = _dilated_attention(x, batch, seq, mix_norm[i], attn_w_in[j], attn_w_out[j],
                                         attn_q_norm[j], attn_k_norm[j])
            x = _ffn(x, ffn2_norm[i], ffn2_w_in[i], ffn2_w_out[i], g2=block_norm[i], pre=mix)
        return x.reshape(batch, seq, d)

    return trunk(x_prompt), trunk(x_sample)
```

```python
import functools
import math

import numpy as np
import jax
import jax.numpy as jnp
from jax import lax
from jax.experimental import pallas as pl
from jax.experimental.pallas import tpu as pltpu

F32 = jnp.float32
BF16 = jnp.bfloat16

NORM_EPS = 1e-6
SGU_GROUPS = 4
SGU_CHUNK = 128
RWKV_HEAD = 64
RWKV_LORA_W = 64
RWKV_GN_EPS = 64e-5
ATTN_HEAD_DIM = 64
ATTN_RADIUS = 64
DILATIONS = (1, 4, 16)
NEG_INF = -1e30
LOG2_E = math.log2(math.e)

LANES = 128
MXU_TILE = 256
SCAN_CHUNK = 64
VMEM_LIMIT_BYTES = 52 * 1024 * 1024


def _params(*semantics):
    return pltpu.CompilerParams(dimension_semantics=semantics, vmem_limit_bytes=VMEM_LIMIT_BYTES)


def _dot(a, b):
    return jnp.dot(a.astype(BF16), b.astype(BF16), preferred_element_type=F32)


def _dot_nt(a, b):
    return lax.dot_general(a.astype(BF16), b.astype(BF16), (((1,), (1,)), ((), ())),
                           preferred_element_type=F32)


def _split(x):
    hi = x.astype(BF16)
    lo = (x - hi.astype(F32)).astype(BF16)
    return hi, lo


def _dot_exact_lhs(lhs_bf16, x):
    hi, lo = _split(x)
    return _dot(lhs_bf16, hi) + _dot(lhs_bf16, lo)


def _sigmoid(x):
    return 0.5 * jnp.tanh(0.5 * x) + 0.5


def _dot_exact_rhs(x, rhs_bf16):
    hi, lo = _split(x)
    return _dot(hi, rhs_bf16) + _dot(lo, rhs_bf16)


def _rms(x, g, eps=NORM_EPS):
    return x * lax.rsqrt(jnp.mean(x * x, axis=-1, keepdims=True) + eps) * g


def _row_tile(n, want):
    t = min(n, want)
    assert n % t == 0, (n, t)
    return t


def _ffn_body(*refs, n_pre, final_norm, f_splits):
    x_ref, g_ref, wg_ref, wu_ref, wo_ref = refs[:5]
    o_ref = refs[-1]
    x = x_ref[...]
    for j in range(n_pre):
        a_ref, wpre_ref = refs[5 + 2 * j:7 + 2 * j]
        x = x + jnp.dot(a_ref[...], wpre_ref[...], preferred_element_type=F32)
    h = _rms(x, g_ref[...]).astype(BF16)
    y = x
    for lo, hi in f_splits:
        gate = jnp.dot(h, wg_ref[:, lo:hi], preferred_element_type=F32)
        up = jnp.dot(h, wu_ref[:, lo:hi], preferred_element_type=F32)
        act = (gate * _sigmoid(gate) * up).astype(BF16)
        y = y + 0.5 * jnp.dot(act, wo_ref[lo:hi, :], preferred_element_type=F32)
    if final_norm:
        y = _rms(y, refs[-2][...])
    o_ref[...] = y


def _ffn(x, g, w_in, w_out, g2=None, pre=()):
    n, d = x.shape
    f_dim = w_out.shape[0]
    tm = _row_tile(n, 512)
    final_norm = g2 is not None
    resident = lambda shape, imap: pl.BlockSpec(shape, imap, pipeline_mode=pl.Buffered(1))
    extra_specs, extra_args = [], []
    for a, w_pre in pre:
        extra_specs += [pl.BlockSpec((tm, a.shape[1]), lambda i: (i, 0)), resident(w_pre.shape, lambda i: (0, 0))]
        extra_args += [a, w_pre.astype(BF16)]
    if final_norm:
        extra_specs.append(pl.BlockSpec((1, d), lambda i: (0, 0)))
        extra_args.append(g2.reshape(1, d))
    assert f_dim % MXU_TILE == 0
    mid = (f_dim // MXU_TILE // 2) * MXU_TILE
    f_splits = ((0, mid), (mid, f_dim))
    w_in = w_in.astype(BF16)
    w_out = w_out.astype(BF16)
    return pl.pallas_call(
        functools.partial(_ffn_body, n_pre=len(pre), final_norm=final_norm, f_splits=f_splits),
        grid=(n // tm,),
        in_specs=[
            pl.BlockSpec((tm, d), lambda i: (i, 0)),
            pl.BlockSpec((1, d), lambda i: (0, 0)),
            resident((d, f_dim), lambda i: (0, 0)),
            resident((d, f_dim), lambda i: (0, 1)),
            resident((f_dim, d), lambda i: (0, 0)),
        ] + extra_specs,
        out_specs=pl.BlockSpec((tm, d), lambda i: (i, 0)),
        out_shape=jax.ShapeDtypeStruct((n, d), F32),
        compiler_params=_params("parallel"),
    )(x, g.reshape(1, d), w_in, w_in, w_out, *extra_args)


def _proj_sgu_body(x_ref, g_ref, w_ref, sgn_ref, sw_ref, sb_ref, ya_ref, rkv_ref, gate_ref, wa_ref,
                   *, d_a, d_b):
    h = _rms(x_ref[...], g_ref[...]).astype(BF16)
    o_rkv, o_g, o_wa = 2 * d_a, 2 * d_a + 3 * d_b, 2 * d_a + 4 * d_b
    u = jnp.dot(h, w_ref[:, 0:d_a], preferred_element_type=F32)
    v = jnp.dot(h, w_ref[:, d_a:2 * d_a], preferred_element_type=F32)
    rkv_ref[...] = jnp.dot(h, w_ref[:, o_rkv:o_g], preferred_element_type=F32)
    gate_ref[...] = jnp.dot(h, w_ref[:, o_g:o_wa], preferred_element_type=F32).astype(BF16)
    wa_ref[...] = jnp.dot(h, w_ref[:, o_wa:], preferred_element_type=F32)
    u = jax.nn.gelu(u)
    v = _rms(jax.nn.gelu(v), sgn_ref[...]).astype(BF16)
    gw = d_a // SGU_GROUPS
    for c in range(x_ref.shape[0] // SGU_CHUNK):
        rows = slice(c * SGU_CHUNK, (c + 1) * SGU_CHUNK)
        for grp in range(SGU_GROUPS):
            cols = slice(grp * gw, (grp + 1) * gw)
            mixed = jnp.dot(sw_ref[grp], v[rows, cols], preferred_element_type=F32) + sb_ref[:, cols]
            ya_ref[rows, cols] = (u[rows, cols] * mixed).astype(BF16)


def _proj_sgu(x, g, w_in, sgu_norm, sgu_w, sgu_b, *, d_a, d_b):
    n, d = x.shape
    d_in = w_in.shape[1]
    d_wa = d_in - 2 * d_a - 4 * d_b
    tm = _row_tile(n, 512)
    gw = d_a // SGU_GROUPS
    bias = jnp.repeat(sgu_b.T, gw, axis=1)
    return pl.pallas_call(
        functools.partial(_proj_sgu_body, d_a=d_a, d_b=d_b),
        grid=(n // tm,),
        in_specs=[
            pl.BlockSpec((tm, d), lambda i: (i, 0)),
            pl.BlockSpec((1, d), lambda i: (0, 0)),
            pl.BlockSpec((d, d_in), lambda i: (0, 0)),
            pl.BlockSpec((1, d_a), lambda i: (0, 0)),
            pl.BlockSpec((SGU_GROUPS, SGU_CHUNK, SGU_CHUNK), lambda i: (0, 0, 0)),
            pl.BlockSpec((SGU_CHUNK, d_a), lambda i: (0, 0)),
        ],
        out_specs=[
            pl.BlockSpec((tm, d_a), lambda i: (i, 0)),
            pl.BlockSpec((tm, 3 * d_b), lambda i: (i, 0)),
            pl.BlockSpec((tm, d_b), lambda i: (i, 0)),
            pl.BlockSpec((tm, d_wa), lambda i: (i, 0)),
        ],
        out_shape=[
            jax.ShapeDtypeStruct((n, d_a), BF16),
            jax.ShapeDtypeStruct((n, 3 * d_b), F32),
            jax.ShapeDtypeStruct((n, d_b), BF16),
            jax.ShapeDtypeStruct((n, d_wa), F32),
        ],
        compiler_params=_params("parallel"),
    )(x, g.reshape(1, d), w_in.astype(BF16), sgu_norm.reshape(1, d_a), sgu_w.astype(BF16), bias)


def _rwkv_scan_body(rkvf_ref, hrf_ref, waf_ref, hwf_ref, rkvb_ref, hrb_ref, wab_ref, hwb_ref,
                    tri_ref, gsum_ref, mu_rkv_ref, mu_wa_ref, w0_ref, wup_ref, a0_ref, aup_ref,
                    kk_ref, ka_ref, rk_ref,
                    yf_ref, bonf_ref, yb_ref, bonb_ref,
                    s_ref, a_s, k_s, b_s, r_s, v_s, ae_s, ke_s, tot_s, *, tb, d_b):
    n = pl.program_id(1)
    nc = tb // SCAN_CHUNK
    assert RWKV_HEAD == SCAN_CHUNK and 2 * RWKV_HEAD == LANES
    first = n == 0
    row = lax.broadcasted_iota(jnp.int32, (tb, 1), 0)
    gsum = gsum_ref[...]

    @pl.when(first)
    def _():
        s_ref[...] = jnp.zeros_like(s_ref)

    def group_sum(x):
        return _dot(x, gsum)

    def prep(d, x, halo, wa, wa_halo, bon_ref):
        halo = jnp.where(first, 0.0, halo)
        wa_halo = jnp.where(first, 0.0, wa_halo)
        if d == 0:
            xs = jnp.where(row == 0, halo, pltpu.roll(x, 1, 0))
            was = jnp.where(row == 0, wa_halo, pltpu.roll(wa, 1, 0))
        else:
            xs = jnp.where(row == tb - 1, halo, pltpu.roll(x, tb - 1, 0))
            was = jnp.where(row == tb - 1, wa_halo, pltpu.roll(wa, tb - 1, 0))
        z = x + (xs - x) * mu_rkv_ref[d:d + 1, :]
        zwa = wa + (was - wa) * mu_wa_ref[d:d + 1, :]
        r, k, v = z[:, 0:d_b], z[:, d_b:2 * d_b], z[:, 2 * d_b:3 * d_b]
        w_raw = w0_ref[d:d + 1, :] + _dot(jnp.tanh(zwa), wup_ref[d])
        lw = -_sigmoid(w_raw) * (math.exp(-0.5) * LOG2_E)
        a = _sigmoid(a0_ref[d:d + 1, :] + _dot(zwa, aup_ref[d]))
        kk = k * kk_ref[...]
        kk = kk * jnp.minimum(lax.rsqrt(group_sum(kk * kk)), 1e12)
        k2 = k * (1.0 + (a - 1.0) * ka_ref[...])
        bon_ref[...] = (group_sum(r * k2 * rk_ref[...]) * v).astype(BF16)
        cums = _dot_exact_lhs(tri_ref[d], lw)
        cum, tot = cums[:tb], cums[tb:]
        e_neg = jnp.exp2(-cum)
        e_end = jnp.exp2(tot - cum)
        kka = kk * a
        a_s[d] = kka * e_neg
        k_s[d] = k2 * e_neg
        b_s[d] = kk * jnp.exp2(cum - lw)
        r_s[d] = r * jnp.exp2(cum)
        v_s[d] = v
        ae_s[d] = kka * e_end
        ke_s[d] = k2 * e_end
        tot_s[d] = tot

    prep(0, rkvf_ref[...], hrf_ref[7:8, :], waf_ref[...], hwf_ref[7:8, :], bonf_ref)
    prep(1, rkvb_ref[...], hrb_ref[0:1, :], wab_ref[...], hwb_ref[0:1, :], bonb_ref)

    y_refs = (yf_ref, yb_ref)
    cc_ = SCAN_CHUNK
    lane = lax.broadcasted_iota(jnp.int32, (1, LANES), 1)
    lo = lane < cc_
    row = lax.broadcasted_iota(jnp.int32, (cc_, LANES), 0)
    col = lax.broadcasted_iota(jnp.int32, (cc_, LANES), 1)
    tok = col & (cc_ - 1)
    diag_lo, diag_hi = col == row, col == row + cc_
    lane2 = lax.broadcasted_iota(jnp.int32, (1, 2 * LANES), 1)
    keep = (lane2 >= cc_, (lane2 < cc_) | (lane2 >= LANES))
    zeros1 = jnp.zeros((cc_, LANES), F32)
    zeros2 = jnp.zeros((cc_, 2 * LANES), F32)
    rows_cat = lambda *xs: jnp.concatenate(xs, axis=0)
    lanes_cat = lambda *xs: jnp.concatenate(xs, axis=1)

    def chunk_step(c, carry):
        pairs = []
        for d in range(2):
            r0 = pl.multiple_of((c if d == 0 else nc - 1 - c) * cc_, cc_)
            rows = pl.ds(r0, cc_)
            strict = (tok < row) if d == 0 else (tok > row)
            incl = (tok <= row) if d == 0 else (tok >= row)
            for p in range(d_b // LANES):
                pairs.append((d, p, rows, slice(p * LANES, (p + 1) * LANES), strict, incl))

        chains = []
        for d, p, rows, cols, strict, incl in pairs:
            bp, rp, vp = b_s[d, rows, cols], r_s[d, rows, cols], v_s[d, rows, cols]
            ap, kp = a_s[d, rows, cols], k_s[d, rows, cols]
            bx, rx = pltpu.roll(bp, cc_, 1), pltpu.roll(rp, cc_, 1)
            zv = lanes_cat(zeros1, vp)
            t = rows_cat(ae_s[d, rows, cols], ke_s[d, rows, cols]).T
            px = pltpu.roll(jnp.exp2(tot_s[d, rows, cols]), cc_, 1)
            for odd in (False, True):
                akt = (rows_cat(kp, ap) if odd else rows_cat(ap, kp)).T
                chains.append(dict(d=d, p=p, odd=odd, mine=~lo if odd else lo, strict=strict, incl=incl,
                                   bp=bp, rp=rp, vp=vp, akt=akt, bx=bx, rx=rx, zv=zv, px=px,
                                   t=t[cc_:] if odd else t[:cc_]))

        for ch in chains:
            mine = ch["mine"]
            ch["g"] = _dot(rows_cat(jnp.where(mine, ch["bp"], 0.0), jnp.where(mine, ch["rp"], 0.0)), ch["akt"])
        for ch in chains:
            ch["gt"] = jnp.where(ch["strict"], ch["g"][:cc_], 0.0)
            ch["w"] = _dot(jnp.where(ch["mine"], 0.0, ch["gt"]),
                           rows_cat(ch["vp"], zeros1) if ch["odd"] else rows_cat(zeros1, ch["vp"]))
        for ch in chains:
            ch["s"] = lanes_cat(jnp.where(ch["mine"], -ch["gt"], ch["bx"]), ch["w"])

        for _ in range(6):
            for ch in chains:
                s = ch["s"]
                rhs = rows_cat(zeros2, s) if ch["odd"] else rows_cat(s, zeros2)
                ch["s"] = _dot(s[:, :LANES], rhs) + jnp.where(keep[ch["odd"]], s, 0.0)

        for ch in chains:
            s, zv = ch["s"], ch["zv"]
            ch["ry"] = _dot(jnp.where(ch["incl"], ch["g"][cc_:], 0.0),
                            rows_cat(zv, -s) if ch["odd"] else rows_cat(-s, zv))
        for ch in chains:
            ch["pp"] = _dot(ch["t"], rows_cat(ch["s"], -ch["zv"]))
        outs = {}
        for ch in chains:
            d, p, odd, ry, pp = ch["d"], ch["p"], ch["odd"], ch["ry"], ch["pp"]
            phi = jnp.where(diag_lo if odd else diag_hi, ch["px"], 0.0) - pp[:, :LANES]
            lhs = rows_cat(ch["rx"] + ry[:, :LANES], phi)
            h0 = s_ref[d, p]
            out = _dot(lhs, rows_cat(h0, zeros1) if odd else rows_cat(zeros1, h0))
            outs[(d, p, odd)] = out + rows_cat(ry[:, LANES:], -pp[:, LANES:])
        for d, p, rows, cols, strict, incl in pairs:
            res = jnp.where(lo, outs[(d, p, False)], outs[(d, p, True)])
            y_refs[d][rows, cols] = res[:cc_].astype(BF16)
            s_ref[d, p] = res[cc_:]
        return carry

    lax.fori_loop(0, nc, chunk_step, 0)


def _rwkv_scan(rkv, wa, batch, seq, mu_rkv, mu_wa, w0, w_up, a0, a_up, k_k, k_a, r_k):
    n, d3 = rkv.shape
    d_b = d3 // 3
    lora = wa.shape[1] // 2
    tb = _row_tile(seq, 256)
    nb = seq // tb
    hb = tb // 8
    last8 = n // 8 - 1

    pos = np.arange(tb)
    same = (pos[:, None] // SCAN_CHUNK) == (pos[None, :] // SCAN_CHUNK)
    tri = np.stack([np.concatenate([same & (pos[None, :] <= pos[:, None]), same]),
                    np.concatenate([same & (pos[None, :] >= pos[:, None]), same])])
    tri = jnp.asarray(tri, BF16)
    ch = np.arange(d_b) // RWKV_HEAD
    gsum = jnp.asarray(ch[:, None] == ch[None, :], BF16)
    wup = jnp.concatenate([w_up, jnp.zeros((2, lora - w_up.shape[1], d_b), F32)], axis=1).astype(BF16)
    aup = jnp.concatenate([jnp.zeros((2, lora - a_up.shape[1], d_b), F32), a_up], axis=1).astype(BF16)

    fwd = lambda b, i: (b * nb + i, 0)
    bwd = lambda b, i: (b * nb + nb - 1 - i, 0)
    fwd_halo = lambda b, i: (jnp.maximum((b * nb + i) * hb - 1, 0), 0)
    bwd_halo = lambda b, i: (jnp.minimum((b * nb + nb - i) * hb, last8), 0)
    const2 = lambda b, i: (0, 0)
    const3 = lambda b, i: (0, 0, 0)
    out_spec_f = pl.BlockSpec((tb, d_b), fwd)
    out_spec_b = pl.BlockSpec((tb, d_b), bwd)
    out_sds = jax.ShapeDtypeStruct((n, d_b), BF16)
    return pl.pallas_call(
        functools.partial(_rwkv_scan_body, tb=tb, d_b=d_b),
        grid=(batch, nb),
        in_specs=[
            pl.BlockSpec((tb, d3), fwd), pl.BlockSpec((8, d3), fwd_halo),
            pl.BlockSpec((tb, lora), fwd), pl.BlockSpec((8, lora), fwd_halo),
            pl.BlockSpec((tb, d3), bwd), pl.BlockSpec((8, d3), bwd_halo),
            pl.BlockSpec((tb, lora), lambda b, i: (b * nb + nb - 1 - i, 1)),
            pl.BlockSpec((8, lora), lambda b, i: (jnp.minimum((b * nb + nb - i) * hb, last8), 1)),
            pl.BlockSpec((2, 2 * tb, tb), const3),
            pl.BlockSpec((d_b, d_b), const2),
            pl.BlockSpec((2, d3), const2),
            pl.BlockSpec((2, lora), const2),
            pl.BlockSpec((2, d_b), const2),
            pl.BlockSpec((2, lora, d_b), const3),
            pl.BlockSpec((2, d_b), const2),
            pl.BlockSpec((2, lora, d_b), const3),
            pl.BlockSpec((1, d_b), const2),
            pl.BlockSpec((1, d_b), const2),
            pl.BlockSpec((1, d_b), const2),
        ],
        out_specs=[out_spec_f, out_spec_f, out_spec_b, out_spec_b],
        out_shape=[out_sds] * 4,
        scratch_shapes=[pltpu.VMEM((2, d_b // LANES, RWKV_HEAD, LANES), F32)]
        + [pltpu.VMEM((2, tb, d_b), F32)] * 8,
        compiler_params=_params("parallel", "arbitrary"),
    )(rkv, rkv, wa, wa, rkv, rkv, wa, wa, tri, gsum, mu_rkv, mu_wa, w0, wup, a0, aup,
      k_k.reshape(1, d_b), k_a.reshape(1, d_b), r_k.reshape(1, d_b))


def _rwkv_post_body(yf_ref, yb_ref, bonf_ref, bonb_ref, gate_ref, gmean_ref, gnw_ref, gnb_ref, o_ref):
    gmean = gmean_ref[...]
    f32 = lambda ref: ref[...].astype(F32)
    y = f32(yf_ref) + f32(yb_ref)
    dev = y - _dot_exact_rhs(y, gmean)
    var = _dot(dev * dev, gmean)
    y = dev * lax.rsqrt(var + RWKV_GN_EPS) * gnw_ref[...] + gnb_ref[...]
    o_ref[...] = ((y + (f32(bonf_ref) + f32(bonb_ref))) * _sigmoid(f32(gate_ref))).astype(BF16)


def _rwkv_post(yf, yb, bonf, bonb, gate, gn_w, gn_b):
    n, d_b = yf.shape
    tm = _row_tile(n, 1024)
    ch = np.arange(d_b) // RWKV_HEAD
    gmean = jnp.asarray((ch[:, None] == ch[None, :]) / RWKV_HEAD, BF16)
    tok = pl.BlockSpec((tm, d_b), lambda i: (i, 0))
    const = lambda r, c: pl.BlockSpec((r, c), lambda i: (0, 0))
    return pl.pallas_call(
        _rwkv_post_body,
        grid=(n // tm,),
        in_specs=[tok] * 5 + [const(d_b, d_b), const(1, d_b), const(1, d_b)],
        out_specs=tok,
        out_shape=jax.ShapeDtypeStruct((n, d_b), BF16),
        compiler_params=_params("parallel"),
    )(yf, yb, bonf, bonb, gate, gmean, gn_w.reshape(1, d_b), gn_b.reshape(1, d_b))


def _attn_proj_body(x_ref, g_ref, w_ref, hmean_ref, qn_ref, kn_ref, q_ref, k_ref, v_ref, *, d_attn):
    h = _rms(x_ref[...], g_ref[...]).astype(BF16)
    hmean = hmean_ref[...]

    def head_rms(t, gain):
        ms = jnp.concatenate(
            [_dot(t[:, c:c + LANES] * t[:, c:c + LANES], hmean) for c in range(0, d_attn, LANES)],
            axis=1)
        return t * lax.rsqrt(ms + NORM_EPS) * gain

    q = jnp.dot(h, w_ref[:, 0:d_attn], preferred_element_type=F32)
    q_ref[...] = head_rms(q, qn_ref[...]) * (ATTN_HEAD_DIM ** -0.5 * LOG2_E)
    k = jnp.dot(h, w_ref[:, d_attn:2 * d_attn], preferred_element_type=F32)
    k_ref[...] = head_rms(k, kn_ref[...])
    v_ref[...] = jnp.dot(h, w_ref[:, 2 * d_attn:], preferred_element_type=F32)


def _attn_proj(x, g, w_in, q_norm, k_norm):
    n, d = x.shape
    d_attn = w_in.shape[1] // 3
    heads = d_attn // ATTN_HEAD_DIM
    tm = _row_tile(n, 512)
    ch = np.arange(LANES) // ATTN_HEAD_DIM
    hmean = jnp.asarray((ch[:, None] == ch[None, :]) / ATTN_HEAD_DIM, BF16)
    tok = pl.BlockSpec((tm, d_attn), lambda i: (i, 0))
    const = lambda r, c: pl.BlockSpec((r, c), lambda i: (0, 0))
    sds = jax.ShapeDtypeStruct((n, d_attn), F32)
    return pl.pallas_call(
        functools.partial(_attn_proj_body, d_attn=d_attn),
        grid=(n // tm,),
        in_specs=[pl.BlockSpec((tm, d), lambda i: (i, 0)), const(1, d), const(d, 3 * d_attn),
                  const(LANES, LANES), const(1, d_attn), const(1, d_attn)],
        out_specs=[tok, tok, tok],
        out_shape=[sds, sds, sds],
        compiler_params=_params("parallel"),
    )(x, g.reshape(1, d), w_in.astype(BF16), hmean,
      jnp.tile(q_norm, heads).reshape(1, d_attn), jnp.tile(k_norm, heads).reshape(1, d_attn))


ATTN_TILE = ATTN_RADIUS * max(DILATIONS)
ATTN_BATCH = 8


def _dilated_attn_body(slopes_ref, q_ref, kp_ref, kc_ref, kn_ref, vp_ref, vc_ref, vn_ref, o_ref,
                       m_s, l_s, acc_s, *, nt):
    it, pair = pl.program_id(1), pl.program_id(2)
    rad = ATTN_RADIUS
    run_blocks = functools.partial(_attn_blocks, q_ref, kp_ref, kc_ref, kn_ref, vp_ref, vc_ref, vn_ref,
                                   m_s, l_s, acc_s)
    qa = lax.broadcasted_iota(jnp.int32, (rad, 3 * rad), 0)
    kj = lax.broadcasted_iota(jnp.int32, (rad, 3 * rad), 1) - rad
    dist = jnp.abs(qa - kj)
    band = dist <= rad
    ok_prev = band & (kj >= jnp.where(it > 0, -rad, 0))
    ok_next = band & (kj < jnp.where(it < nt - 1, 2 * rad, rad))
    ok_both = ok_prev & ok_next

    for d in DILATIONS:
        nsub = ATTN_TILE // (rad * d)
        dist_f = (dist * d).astype(F32)
        biases = []
        for hh in range(LANES // ATTN_HEAD_DIM):
            base = (-slopes_ref[pair * (LANES // ATTN_HEAD_DIM) + hh] * dist_f) * LOG2_E
            biases.append({(False, False): jnp.where(band, base, NEG_INF),
                           (True, False): jnp.where(ok_prev, base, NEG_INF),
                           (False, True): jnp.where(ok_next, base, NEG_INF),
                           (True, True): jnp.where(ok_both, base, NEG_INF)})
        specs = [(r, i) for r in range(d) for i in range(nsub)]
        for b0 in range(0, len(specs), ATTN_BATCH):
            run_blocks(d, nsub, specs[b0:b0 + ATTN_BATCH], biases)
    o_ref[...] = (acc_s[...] / l_s[...]).astype(BF16)


def _attn_blocks(q_ref, kp_ref, kc_ref, kn_ref, vp_ref, vc_ref, vn_ref, m_s, l_s, acc_s, d, nsub, specs, biases):
    rad = ATTN_RADIUS
    lo = lax.broadcasted_iota(jnp.int32, (1, LANES), 1) < ATTN_HEAD_DIM

    def strided(start, d):
        return pl.ds(start, rad, stride=d) if d > 1 else pl.ds(start, rad)

    blocks, pieces = [], {}
    for r, i in specs:
        kps, vps = [], []
        for piece in (-1, 0, 1):
            off = rad * d * (i + piece) + r
            if off < 0:
                kr, vr, off = kp_ref, vp_ref, off + ATTN_TILE
            elif off >= ATTN_TILE:
                kr, vr, off = kn_ref, vn_ref, off - ATTN_TILE
            else:
                kr, vr = kc_ref, vc_ref
            key = (kr is kp_ref, kr is kn_ref, off)
            if key not in pieces:
                pieces[key] = (kr[strided(off, d), :].astype(BF16), vr[strided(off, d), :].astype(BF16))
            kps.append(pieces[key][0])
            vps.append(pieces[key][1])
        rows = strided(r + rad * d * i, d)
        blocks.append(dict(rows=rows, edge=(i == 0, i == nsub - 1), q=q_ref[rows, :],
                           kw=jnp.concatenate(kps, axis=0), vw=jnp.concatenate(vps, axis=0)))
    for blk in blocks:
        q2 = jnp.concatenate([jnp.where(lo, blk["q"], 0.0), jnp.where(lo, 0.0, blk["q"])], axis=0)
        blk["s"] = _dot_nt(q2, blk["kw"]) + jnp.concatenate([b[blk["edge"]] for b in biases], axis=0)
    for blk in blocks:
        blk["m"] = jnp.max(blk["s"], axis=-1, keepdims=True)
    for blk in blocks:
        blk["p"] = jnp.exp2(blk["s"] - blk["m"])
    for blk in blocks:
        blk["l"] = jnp.sum(blk["p"], axis=-1, keepdims=True)
        blk["acc"] = _dot(blk["p"], blk["vw"])
    for blk in blocks:
        rows = blk["rows"]
        m_new = jnp.where(lo, blk["m"][:rad], blk["m"][rad:])
        l_new = jnp.where(lo, blk["l"][:rad], blk["l"][rad:])
        acc_new = jnp.where(lo, blk["acc"][:rad], blk["acc"][rad:])
        if d == DILATIONS[0]:
            m_s[rows, :], l_s[rows, :], acc_s[rows, :] = m_new, l_new, acc_new
        else:
            m_old = m_s[rows, :]
            m_tot = jnp.maximum(m_old, m_new)
            w_old, w_new = jnp.exp2(m_old - m_tot), jnp.exp2(m_new - m_tot)
            m_s[rows, :] = m_tot
            l_s[rows, :] = w_old * l_s[rows, :] + w_new * l_new
            acc_s[rows, :] = w_old * acc_s[rows, :] + w_new * acc_new


def _dilated_attn(q, k, v, slopes, batch, seq):
    n, d_attn = q.shape
    assert seq % ATTN_TILE == 0
    nt = seq // ATTN_TILE
    last = n // ATTN_TILE - 1
    cur = pl.BlockSpec((ATTN_TILE, LANES), lambda b, i, p: (b * nt + i, p))
    prev = pl.BlockSpec((ATTN_TILE, LANES), lambda b, i, p: (jnp.maximum(b * nt + i - 1, 0), p))
    nxt = pl.BlockSpec((ATTN_TILE, LANES), lambda b, i, p: (jnp.minimum(b * nt + i + 1, last), p))
    return pl.pallas_call(
        functools.partial(_dilated_attn_body, nt=nt),
        grid=(batch, nt, d_attn // LANES),
        in_specs=[pl.BlockSpec(memory_space=pltpu.SMEM), cur, prev, cur, nxt, prev, cur, nxt],
        out_specs=cur,
        out_shape=jax.ShapeDtypeStruct((n, d_attn), BF16),
        scratch_shapes=[pltpu.VMEM((ATTN_TILE, LANES), F32)] * 3,
        compiler_params=_params("parallel", "parallel", "arbitrary"),
    )(slopes, q, k, k, k, v, v, v)


def _dilated_attention(x, batch, seq, g, w_in, w_out, q_norm, k_norm):
    q, k, v = _attn_proj(x, g, w_in, q_norm, k_norm)
    heads = q.shape[1] // ATTN_HEAD_DIM
    slopes = 2.0 ** (-8.0 * jnp.arange(1, heads + 1, dtype=F32) / heads)
    return [(_dilated_attn(q, k, v, slopes, batch, seq), w_out)]


def _sgu_rwkv_mixer(x, batch, seq, g, w_in, w_out, sgu_norm, sgu_w, sgu_b, mu_rkv, mu_wa, w0, w_up,
                    a0, a_up, k_k, k_a, r_k, gn_w, gn_b):
    d_a = sgu_norm.shape[0]
    d_b = k_k.shape[0]
    ya, rkv, gate, wa = _proj_sgu(x, g, w_in, sgu_norm, sgu_w, sgu_b, d_a=d_a, d_b=d_b)
    yf, bonf, yb, bonb = _rwkv_scan(rkv, wa, batch, seq, mu_rkv, mu_wa, w0, w_up, a0, a_up,
                                    k_k, k_a, r_k.reshape(-1))
    y_rwkv = _rwkv_post(yf, yb, bonf, bonb, gate, gn_w, gn_b)
    return [(ya, w_out[:d_a]), (y_rwkv, w_out[d_a:])]


def kernel(x_prompt, x_sample, ffn1_norm, ffn1_w_in, ffn1_w_out, mix_norm, ffn2_norm, ffn2_w_in,
           ffn2_w_out, block_norm, ab_w_in, ab_w_out, sgu_norm, sgu_w, sgu_b, rwkv_mu_rkv, rwkv_mu_wa,
           rwkv_w0, rwkv_w_up, rwkv_a0, rwkv_a_up, rwkv_k_k, rwkv_k_a, rwkv_r_k, rwkv_gn_w, rwkv_gn_b,
           attn_w_in, attn_w_out, attn_q_norm, attn_k_norm):
    depth = ffn1_norm.shape[0]

    def trunk(x3):
        batch, seq, d = x3.shape
        x = x3.reshape(batch * seq, d)
        for i in range(depth):
            j = i // 2
            x = _ffn(x, ffn1_norm[i], ffn1_w_in[i], ffn1_w_out[i])
            if i % 2 == 0:
                mix = _sgu_rwkv_mixer(x, batch, seq, mix_norm[i], ab_w_in[j], ab_w_out[j], sgu_norm[j],
                                      sgu_w[j], sgu_b[j], rwkv_mu_rkv[j], rwkv_mu_wa[j], rwkv_w0[j],
                                      rwkv_w_up[j], rwkv_a0[j], rwkv_a_up[j], rwkv_k_k[j], rwkv_k_a[j],
                                      rwkv_r_k[j], rwkv_gn_w[j], rwkv_gn_b[j])
            else:
                mix = _dilated_attention(x, batch, seq, mix_norm[i], attn_w_in[j], attn_w_out[j],
                                         attn_q_norm[j], attn_k_norm[j])
            x = _ffn(x, ffn2_norm[i], ffn2_w_in[i], ffn2_w_out[i], g2=block_norm[i], pre=mix)
        return x.reshape(batch, seq, d)

    return trunk(x_prompt), trunk(x_sample)
```

```python
import functools
import math

import numpy as np
import jax
import jax.numpy as jnp
from jax import lax
from jax.experimental import pallas as pl
from jax.experimental.pallas import tpu as pltpu

F32 = jnp.float32
BF16 = jnp.bfloat16

NORM_EPS = 1e-6
SGU_GROUPS = 4
SGU_CHUNK = 128
RWKV_HEAD = 64
RWKV_LORA_W = 64
RWKV_GN_EPS = 64e-5
ATTN_HEAD_DIM = 64
ATTN_RADIUS = 64
DILATIONS = (1, 4, 16)
NEG_INF = -1e30
LOG2_E = math.log2(math.e)

LANES = 128
MXU_TILE = 256
SCAN_CHUNK = 64
VMEM_LIMIT_BYTES = 52 * 1024 * 1024


def _params(*semantics):
    return pltpu.CompilerParams(dimension_semantics=semantics, vmem_limit_bytes=VMEM_LIMIT_BYTES)


def _dot(a, b):
    return jnp.dot(a.astype(BF16), b.astype(BF16), preferred_element_type=F32)


def _dot_nt(a, b):
    return lax.dot_general(a.astype(BF16), b.astype(BF16), (((1,), (1,)), ((), ())),
                           preferred_element_type=F32)


def _split(x):
    hi = x.astype(BF16)
    lo = (x - hi.astype(F32)).astype(BF16)
    return hi, lo


def _dot_exact_lhs(lhs_bf16, x):
    hi, lo = _split(x)
    return _dot(lhs_bf16, hi) + _dot(lhs_bf16, lo)


def _sigmoid(x):
    return 0.5 * jnp.tanh(0.5 * x) + 0.5


def _dot_exact_rhs(x, rhs_bf16):
    hi, lo = _split(x)
    return _dot(hi, rhs_bf16) + _dot(lo, rhs_bf16)


def _rms(x, g, eps=NORM_EPS):
    return x * lax.rsqrt(jnp.mean(x * x, axis=-1, keepdims=True) + eps) * g


def _row_tile(n, want):
    t = min(n, want)
    assert n % t == 0, (n, t)
    return t


def _ffn_body(*refs, n_pre, final_norm, f_splits):
    x_ref, g_ref, wg_ref, wu_ref, wo_ref = refs[:5]
    o_ref = refs[-1]
    x = x_ref[...]
    for j in range(n_pre):
        a_ref, wpre_ref = refs[5 + 2 * j:7 + 2 * j]
        x = x + jnp.dot(a_ref[...], wpre_ref[...], preferred_element_type=F32)
    h = _rms(x, g_ref[...]).astype(BF16)
    y = x
    for lo, hi in f_splits:
        gate = jnp.dot(h, wg_ref[:, lo:hi], preferred_element_type=F32)
        up = jnp.dot(h, wu_ref[:, lo:hi], preferred_element_type=F32)
        act = (gate * _sigmoid(gate) * up).astype(BF16)
        y = y + 0.5 * jnp.dot(act, wo_ref[lo:hi, :], preferred_element_type=F32)
    if final_norm:
        y = _rms(y, refs[-2][...])
    o_ref[...] = y


def _ffn(x, g, w_in, w_out, g2=None, pre=()):
    n, d = x.shape
    f_dim = w_out.shape[0]
    tm = _row_tile(n, 512)
    final_norm = g2 is not None
    resident = lambda shape, imap: pl.BlockSpec(shape, imap, pipeline_mode=pl.Buffered(1))
    extra_specs, extra_args = [], []
    for a, w_pre in pre:
        extra_specs += [pl.BlockSpec((tm, a.shape[1]), lambda i: (i, 0)), resident(w_pre.shape, lambda i: (0, 0))]
        extra_args += [a, w_pre.astype(BF16)]
    if final_norm:
        extra_specs.append(pl.BlockSpec((1, d), lambda i: (0, 0)))
        extra_args.append(g2.reshape(1, d))
    assert f_dim % MXU_TILE == 0
    mid = (f_dim // MXU_TILE // 2) * MXU_TILE
    f_splits = ((0, mid), (mid, f_dim))
    w_in = w_in.astype(BF16)
    w_out = w_out.astype(BF16)
    return pl.pallas_call(
        functools.partial(_ffn_body, n_pre=len(pre), final_norm=final_norm, f_splits=f_splits),
        grid=(n // tm,),
        in_specs=[
            pl.BlockSpec((tm, d), lambda i: (i, 0)),
            pl.BlockSpec((1, d), lambda i: (0, 0)),
            resident((d, f_dim), lambda i: (0, 0)),
            resident((d, f_dim), lambda i: (0, 1)),
            resident((f_dim, d), lambda i: (0, 0)),
        ] + extra_specs,
        out_specs=pl.BlockSpec((tm, d), lambda i: (i, 0)),
        out_shape=jax.ShapeDtypeStruct((n, d), F32),
        compiler_params=_params("parallel"),
    )(x, g.reshape(1, d), w_in, w_in, w_out, *extra_args)


def _proj_sgu_body(x_ref, g_ref, w_ref, sgn_ref, sw_ref, sb_ref, ya_ref, rkv_ref, gate_ref, wa_ref,
                   *, d_a, d_b):
    h = _rms(x_ref[...], g_ref[...]).astype(BF16)
    o_rkv, o_g, o_wa = 2 * d_a, 2 * d_a + 3 * d_b, 2 * d_a + 4 * d_b
    u = jnp.dot(h, w_ref[:, 0:d_a], preferred_element_type=F32)
    v = jnp.dot(h, w_ref[:, d_a:2 * d_a], preferred_element_type=F32)
    rkv_ref[...] = jnp.dot(h, w_ref[:, o_rkv:o_g], preferred_element_type=F32)
    gate_ref[...] = jnp.dot(h, w_ref[:, o_g:o_wa], preferred_element_type=F32).astype(BF16)
    wa_ref[...] = jnp.dot(h, w_ref[:, o_wa:], preferred_element_type=F32)
    u = jax.nn.gelu(u)
    v = _rms(jax.nn.gelu(v), sgn_ref[...]).astype(BF16)
    gw = d_a // SGU_GROUPS
    for c in range(x_ref.shape[0] // SGU_CHUNK):
        rows = slice(c * SGU_CHUNK, (c + 1) * SGU_CHUNK)
        for grp in range(SGU_GROUPS):
            cols = slice(grp * gw, (grp + 1) * gw)
            mixed = jnp.dot(sw_ref[grp], v[rows, cols], preferred_element_type=F32) + sb_ref[:, cols]
            ya_ref[rows, cols] = (u[rows, cols] * mixed).astype(BF16)


def _proj_sgu(x, g, w_in, sgu_norm, sgu_w, sgu_b, *, d_a, d_b):
    n, d = x.shape
    d_in = w_in.shape[1]
    d_wa = d_in - 2 * d_a - 4 * d_b
    tm = _row_tile(n, 512)
    gw = d_a // SGU_GROUPS
    bias = jnp.repeat(sgu_b.T, gw, axis=1)
    return pl.pallas_call(
        functools.partial(_proj_sgu_body, d_a=d_a, d_b=d_b),
        grid=(n // tm,),
        in_specs=[
            pl.BlockSpec((tm, d), lambda i: (i, 0)),
            pl.BlockSpec((1, d), lambda i: (0, 0)),
            pl.BlockSpec((d, d_in), lambda i: (0, 0)),
            pl.BlockSpec((1, d_a), lambda i: (0, 0)),
            pl.BlockSpec((SGU_GROUPS, SGU_CHUNK, SGU_CHUNK), lambda i: (0, 0, 0)),
            pl.BlockSpec((SGU_CHUNK, d_a), lambda i: (0, 0)),
        ],
        out_specs=[
            pl.BlockSpec((tm, d_a), lambda i: (i, 0)),
            pl.BlockSpec((tm, 3 * d_b), lambda i: (i, 0)),
            pl.BlockSpec((tm, d_b), lambda i: (i, 0)),
            pl.BlockSpec((tm, d_wa), lambda i: (i, 0)),
        ],
        out_shape=[
            jax.ShapeDtypeStruct((n, d_a), BF16),
            jax.ShapeDtypeStruct((n, 3 * d_b), F32),
            jax.ShapeDtypeStruct((n, d_b), BF16),
            jax.ShapeDtypeStruct((n, d_wa), F32),
        ],
        compiler_params=_params("parallel"),
    )(x, g.reshape(1, d), w_in.astype(BF16), sgu_norm.reshape(1, d_a), sgu_w.astype(BF16), bias)


def _rwkv_scan_body(rkvf_ref, hrf_ref, waf_ref, hwf_ref, rkvb_ref, hrb_ref, wab_ref, hwb_ref,
                    tri_ref, gsum_ref, mu_rkv_ref, mu_wa_ref, w0_ref, wup_ref, a0_ref, aup_ref,
                    kk_ref, ka_ref, rk_ref,
                    yf_ref, bonf_ref, yb_ref, bonb_ref,
                    s_ref, a_s, k_s, b_s, r_s, v_s, ae_s, ke_s, tot_s, *, tb, d_b):
    n = pl.program_id(1)
    nc = tb // SCAN_CHUNK
    assert RWKV_HEAD == SCAN_CHUNK and 2 * RWKV_HEAD == LANES
    first = n == 0
    row = lax.broadcasted_iota(jnp.int32, (tb, 1), 0)
    gsum = gsum_ref[...]

    @pl.when(first)
    def _():
        s_ref[...] = jnp.zeros_like(s_ref)

    def group_sum(x):
        return _dot(x, gsum)

    def prep(d, x, halo, wa, wa_halo, bon_ref):
        halo = jnp.where(first, 0.0, halo)
        wa_halo = jnp.where(first, 0.0, wa_halo)
        if d == 0:
            xs = jnp.where(row == 0, halo, pltpu.roll(x, 1, 0))
            was = jnp.where(row == 0, wa_halo, pltpu.roll(wa, 1, 0))
        else:
            xs = jnp.where(row == tb - 1, halo, pltpu.roll(x, tb - 1, 0))
            was = jnp.where(row == tb - 1, wa_halo, pltpu.roll(wa, tb - 1, 0))
        z = x + (xs - x) * mu_rkv_ref[d:d + 1, :]
        zwa = wa + (was - wa) * mu_wa_ref[d:d + 1, :]
        r, k, v = z[:, 0:d_b], z[:, d_b:2 * d_b], z[:, 2 * d_b:3 * d_b]
        w_raw = w0_ref[d:d + 1, :] + _dot(jnp.tanh(zwa), wup_ref[d])
        lw = -_sigmoid(w_raw) * (math.exp(-0.5) * LOG2_E)
        a = _sigmoid(a0_ref[d:d + 1, :] + _dot(zwa, aup_ref[d]))
        kk = k * kk_ref[...]
        kk = kk * jnp.minimum(lax.rsqrt(group_sum(kk * kk)), 1e12)
        k2 = k * (1.0 + (a - 1.0) * ka_ref[...])
        bon_ref[...] = (group_sum(r * k2 * rk_ref[...]) * v).astype(BF16)
        cums = _dot_exact_lhs(tri_ref[d], lw)
        cum, tot = cums[:tb], cums[tb:]
        e_neg = jnp.exp2(-cum)
        e_end = jnp.exp2(tot - cum)
        kka = kk * a
        a_s[d] = kka * e_neg
        k_s[d] = k2 * e_neg
        b_s[d] = kk * jnp.exp2(cum - lw)
        r_s[d] = r * jnp.exp2(cum)
        v_s[d] = v
        ae_s[d] = kka * e_end
        ke_s[d] = k2 * e_end
        tot_s[d] = tot

    prep(0, rkvf_ref[...], hrf_ref[7:8, :], waf_ref[...], hwf_ref[7:8, :], bonf_ref)
    prep(1, rkvb_ref[...], hrb_ref[0:1, :], wab_ref[...], hwb_ref[0:1, :], bonb_ref)

    y_refs = (yf_ref, yb_ref)
    cc_ = SCAN_CHUNK
    lane = lax.broadcasted_iota(jnp.int32, (1, LANES), 1)
    lo = lane < cc_
    row = lax.broadcasted_iota(jnp.int32, (cc_, LANES), 0)
    col = lax.broadcasted_iota(jnp.int32, (cc_, LANES), 1)
    tok = col & (cc_ - 1)
    diag_lo, diag_hi = col == row, col == row + cc_
    lane2 = lax.broadcasted_iota(jnp.int32, (1, 2 * LANES), 1)
    keep = (lane2 >= cc_, (lane2 < cc_) | (lane2 >= LANES))
    zeros1 = jnp.zeros((cc_, LANES), F32)
    zeros2 = jnp.zeros((cc_, 2 * LANES), F32)
    rows_cat = lambda *xs: jnp.concatenate(xs, axis=0)
    lanes_cat = lambda *xs: jnp.concatenate(xs, axis=1)

    def chunk_step(c, carry):
        pairs = []
        for d in range(2):
            r0 = pl.multiple_of((c if d == 0 else nc - 1 - c) * cc_, cc_)
            rows = pl.ds(r0, cc_)
            strict = (tok < row) if d == 0 else (tok > row)
            incl = (tok <= row) if d == 0 else (tok >= row)
            for p in range(d_b // LANES):
                pairs.append((d, p, rows, slice(p * LANES, (p + 1) * LANES), strict, incl))

        chains = []
        for d, p, rows, cols, strict, incl in pairs:
            bp, rp, vp = b_s[d, rows, cols], r_s[d, rows, cols], v_s[d, rows, cols]
            ap, kp = a_s[d, rows, cols], k_s[d, rows, cols]
            bx, rx = pltpu.roll(bp, cc_, 1), pltpu.roll(rp, cc_, 1)
            zv = lanes_cat(zeros1, vp)
            t = rows_cat(ae_s[d, rows, cols], ke_s[d, rows, cols]).T
            px = pltpu.roll(jnp.exp2(tot_s[d, rows, cols]), cc_, 1)
            for odd in (False, True):
                akt = (rows_cat(kp, ap) if odd else rows_cat(ap, kp)).T
                chains.append(dict(d=d, p=p, odd=odd, mine=~lo if odd else lo, strict=strict, incl=incl,
                                   bp=bp, rp=rp, vp=vp, akt=akt, bx=bx, rx=rx, zv=zv, px=px,
                                   t=t[cc_:] if odd else t[:cc_]))

        for ch in chains:
            mine = ch["mine"]
            ch["g"] = _dot(rows_cat(jnp.where(mine, ch["bp"], 0.0), jnp.where(mine, ch["rp"], 0.0)), ch["akt"])
        for ch in chains:
            ch["gt"] = jnp.where(ch["strict"], ch["g"][:cc_], 0.0)
            ch["w"] = _dot(jnp.where(ch["mine"], 0.0, ch["gt"]),
                           rows_cat(ch["vp"], zeros1) if ch["odd"] else rows_cat(zeros1, ch["vp"]))
        for ch in chains:
            ch["s"] = lanes_cat(jnp.where(ch["mine"], -ch["gt"], ch["bx"]), ch["w"])

        for _ in range(6):
            for ch in chains:
                s = ch["s"]
                rhs = rows_cat(zeros2, s) if ch["odd"] else rows_cat(s, zeros2)
                ch["s"] = _dot(s[:, :LANES], rhs) + jnp.where(keep[ch["odd"]], s, 0.0)

        for ch in chains:
            s, zv = ch["s"], ch["zv"]
            ch["ry"] = _dot(jnp.where(ch["incl"], ch["g"][cc_:], 0.0),
                            rows_cat(zv, -s) if ch["odd"] else rows_cat(-s, zv))
        for ch in chains:
            ch["pp"] = _dot(ch["t"], rows_cat(ch["s"], -ch["zv"]))
        outs = {}
        for ch in chains:
            d, p, odd, ry, pp = ch["d"], ch["p"], ch["odd"], ch["ry"], ch["pp"]
            phi = jnp.where(diag_lo if odd else diag_hi, ch["px"], 0.0) - pp[:, :LANES]
            lhs = rows_cat(ch["rx"] + ry[:, :LANES], phi)
            h0 = s_ref[d, p]
            out = _dot(lhs, rows_cat(h0, zeros1) if odd else rows_cat(zeros1, h0))
            outs[(d, p, odd)] = out + rows_cat(ry[:, LANES:], -pp[:, LANES:])
        for d, p, rows, cols, strict, incl in pairs:
            res = jnp.where(lo, outs[(d, p, False)], outs[(d, p, True)])
            y_refs[d][rows, cols] = res[:cc_].astype(BF16)
            s_ref[d, p] = res[cc_:]
        return carry

    lax.fori_loop(0, nc, chunk_step, 0)


def _rwkv_scan(rkv, wa, batch, seq, mu_rkv, mu_wa, w0, w_up, a0, a_up, k_k, k_a, r_k):
    n, d3 = rkv.shape
    d_b = d3 // 3
    lora = wa.shape[1] // 2
    tb = _row_tile(seq, 256)
    nb = seq // tb
    hb = tb // 8
    last8 = n // 8 - 1

    pos = np.arange(tb)
    same = (pos[:, None] // SCAN_CHUNK) == (pos[None, :] // SCAN_CHUNK)
    tri = np.stack([np.concatenate([same & (pos[None, :] <= pos[:, None]), same]),
                    np.concatenate([same & (pos[None, :] >= pos[:, None]), same])])
    tri = jnp.asarray(tri, BF16)
    ch = np.arange(d_b) // RWKV_HEAD
    gsum = jnp.asarray(ch[:, None] == ch[None, :], BF16)
    wup = jnp.concatenate([w_up, jnp.zeros((2, lora - w_up.shape[1], d_b), F32)], axis=1).astype(BF16)
    aup = jnp.concatenate([jnp.zeros((2, lora - a_up.shape[1], d_b), F32), a_up], axis=1).astype(BF16)

    fwd = lambda b, i: (b * nb + i, 0)
    bwd = lambda b, i: (b * nb + nb - 1 - i, 0)
    fwd_halo = lambda b, i: (jnp.maximum((b * nb + i) * hb - 1, 0), 0)
    bwd_halo = lambda b, i: (jnp.minimum((b * nb + nb - i) * hb, last8), 0)
    const2 = lambda b, i: (0, 0)
    const3 = lambda b, i: (0, 0, 0)
    out_spec_f = pl.BlockSpec((tb, d_b), fwd)
    out_spec_b = pl.BlockSpec((tb, d_b), bwd)
    out_sds = jax.ShapeDtypeStruct((n, d_b), BF16)
    return pl.pallas_call(
        functools.partial(_rwkv_scan_body, tb=tb, d_b=d_b),
        grid=(batch, nb),
        in_specs=[
            pl.BlockSpec((tb, d3), fwd), pl.BlockSpec((8, d3), fwd_halo),
            pl.BlockSpec((tb, lora), fwd), pl.BlockSpec((8, lora), fwd_halo),
            pl.BlockSpec((tb, d3), bwd), pl.BlockSpec((8, d3), bwd_halo),
            pl.BlockSpec((tb, lora), lambda b, i: (b * nb + nb - 1 - i, 1)),
            pl.BlockSpec((8, lora), lambda b, i: (jnp.minimum((b * nb + nb - i) * hb, last8), 1)),
            pl.BlockSpec((2, 2 * tb, tb), const3),
            pl.BlockSpec((d_b, d_b), const2),
            pl.BlockSpec((2, d3), const2),
            pl.BlockSpec((2, lora), const2),
            pl.BlockSpec((2, d_b), const2),
            pl.BlockSpec((2, lora, d_b), const3),
            pl.BlockSpec((2, d_b), const2),
            pl.BlockSpec((2, lora, d_b), const3),
            pl.BlockSpec((1, d_b), const2),
            pl.BlockSpec((1, d_b), const2),
            pl.BlockSpec((1, d_b), const2),
        ],
        out_specs=[out_spec_f, out_spec_f, out_spec_b, out_spec_b],
        out_shape=[out_sds] * 4,
        scratch_shapes=[pltpu.VMEM((2, d_b // LANES, RWKV_HEAD, LANES), F32)]
        + [pltpu.VMEM((2, tb, d_b), F32)] * 8,
        compiler_params=_params("parallel", "arbitrary"),
    )(rkv, rkv, wa, wa, rkv, rkv, wa, wa, tri, gsum, mu_rkv, mu_wa, w0, wup, a0, aup,
      k_k.reshape(1, d_b), k_a.reshape(1, d_b), r_k.reshape(1, d_b))


def _rwkv_post_body(yf_ref, yb_ref, bonf_ref, bonb_ref, gate_ref, gmean_ref, gnw_ref, gnb_ref, o_ref):
    gmean = gmean_ref[...]
    f32 = lambda ref: ref[...].astype(F32)
    y = f32(yf_ref) + f32(yb_ref)
    dev = y - _dot_exact_rhs(y, gmean)
    var = _dot(dev * dev, gmean)
    y = dev * lax.rsqrt(var + RWKV_GN_EPS) * gnw_ref[...] + gnb_ref[...]
    o_ref[...] = ((y + (f32(bonf_ref) + f32(bonb_ref))) * _sigmoid(f32(gate_ref))).astype(BF16)


def _rwkv_post(yf, yb, bonf, bonb, gate, gn_w, gn_b):
    n, d_b = yf.shape
    tm = _row_tile(n, 1024)
    ch = np.arange(d_b) // RWKV_HEAD
    gmean = jnp.asarray((ch[:, None] == ch[None, :]) / RWKV_HEAD, BF16)
    tok = pl.BlockSpec((tm, d_b), lambda i: (i, 0))
    const = lambda r, c: pl.BlockSpec((r, c), lambda i: (0, 0))
    return pl.pallas_call(
        _rwkv_post_body,
        grid=(n // tm,),
        in_specs=[tok] * 5 + [const(d_b, d_b), const(1, d_b), const(1, d_b)],
        out_specs=tok,
        out_shape=jax.ShapeDtypeStruct((n, d_b), BF16),
        compiler_params=_params("parallel"),
    )(yf, yb, bonf, bonb, gate, gmean, gn_w.reshape(1, d_b), gn_b.reshape(1, d_b))


def _attn_proj_body(x_ref, g_ref, w_ref, hmean_ref, qn_ref, kn_ref, q_ref, k_ref, v_ref, *, d_attn):
    h = _rms(x_ref[...], g_ref[...]).astype(BF16)
    hmean = hmean_ref[...]

    def head_rms(t, gain):
        ms = jnp.concatenate(
            [_dot(t[:, c:c + LANES] * t[:, c:c + LANES], hmean) for c in range(0, d_attn, LANES)],
            axis=1)
        return t * lax.rsqrt(ms + NORM_EPS) * gain

    q = jnp.dot(h, w_ref[:, 0:d_attn], preferred_element_type=F32)
    q_ref[...] = head_rms(q, qn_ref[...]) * (ATTN_HEAD_DIM ** -0.5 * LOG2_E)
    k = jnp.dot(h, w_ref[:, d_attn:2 * d_attn], preferred_element_type=F32)
    k_ref[...] = head_rms(k, kn_ref[...])
    v_ref[...] = jnp.dot(h, w_ref[:, 2 * d_attn:], preferred_element_type=F32)


def _attn_proj(x, g, w_in, q_norm, k_norm):
    n, d = x.shape
    d_attn = w_in.shape[1] // 3
    heads = d_attn // ATTN_HEAD_DIM
    tm = _row_tile(n, 512)
    ch = np.arange(LANES) // ATTN_HEAD_DIM
    hmean = jnp.asarray((ch[:, None] == ch[None, :]) / ATTN_HEAD_DIM, BF16)
    tok = pl.BlockSpec((tm, d_attn), lambda i: (i, 0))
    const = lambda r, c: pl.BlockSpec((r, c), lambda i: (0, 0))
    sds = jax.ShapeDtypeStruct((n, d_attn), F32)
    return pl.pallas_call(
        functools.partial(_attn_proj_body, d_attn=d_attn),
        grid=(n // tm,),
        in_specs=[pl.BlockSpec((tm, d), lambda i: (i, 0)), const(1, d), const(d, 3 * d_attn),
                  const(LANES, LANES), const(1, d_attn), const(1, d_attn)],
        out_specs=[tok, tok, tok],
        out_shape=[sds, sds, sds],
        compiler_params=_params("parallel"),
    )(x, g.reshape(1, d), w_in.astype(BF16), hmean,
      jnp.tile(q_norm, heads).reshape(1, d_attn), jnp.tile(k_norm, heads).reshape(1, d_attn))


ATTN_TILE = ATTN_RADIUS * max(DILATIONS)
ATTN_SPLIT = 4
ATTN_BATCH = 8


def _dilated_attn_body(slopes_ref, q_ref, kp_ref, kc_ref, kn_ref, vp_ref, vc_ref, vn_ref, o_ref,
                       m_s, l_s, acc_s, qd, kd, vd, m2_s, l2_s, acc2_s, *, nt):
    it, pair = pl.program_id(1), pl.program_id(2)
    rad = ATTN_RADIUS
    sub, slab = ATTN_SPLIT, ATTN_TILE // ATTN_SPLIT
    for r4 in range(sub):
        rows = pl.ds(r4, slab, stride=sub)
        qd[r4] = q_ref[rows, :]
        for w, (kr, vr) in enumerate(((kp_ref, vp_ref), (kc_ref, vc_ref), (kn_ref, vn_ref))):
            kd[w * sub + r4] = kr[rows, :]
            vd[w * sub + r4] = vr[rows, :]
    nat = (m_s, l_s, acc_s)
    cls = (m2_s, l2_s, acc2_s)

    def accessors(d):
        if d == 1:
            def kv(r, i, piece):
                off = rad * (i + piece)
                kr, vr = (kp_ref, vp_ref) if off < 0 else (kn_ref, vn_ref) if off >= ATTN_TILE else (kc_ref, vc_ref)
                rows = pl.ds(off % ATTN_TILE, rad)
                return kr[rows, :], vr[rows, :]
            return (lambda r, i: q_ref[pl.ds(rad * i, rad), :]), kv, (lambda r, i: (nat, (pl.ds(rad * i, rad),))), True
        if d == sub:
            def kv(r, i, piece):
                off = rad * (i + piece)
                w = 0 if off < 0 else 2 if off >= slab else 1
                rows = pl.ds(off % slab, rad)
                return kd[w * sub + r, rows, :], vd[w * sub + r, rows, :]
            return (lambda r, i: qd[r, pl.ds(rad * i, rad), :]), kv, (lambda r, i: (cls, (r, pl.ds(rad * i, rad)))), True
        rows16 = lambda r: pl.ds(r // sub, rad, stride=d // sub)
        kv = lambda r, i, piece: (kd[(piece + 1) * sub + r % sub, rows16(r), :], vd[(piece + 1) * sub + r % sub, rows16(r), :])
        return (lambda r, i: qd[r % sub, rows16(r), :]), kv, (lambda r, i: (cls, (r % sub, rows16(r)))), False

    qa = lax.broadcasted_iota(jnp.int32, (rad, 3 * rad), 0)
    kj = lax.broadcasted_iota(jnp.int32, (rad, 3 * rad), 1) - rad
    dist = jnp.abs(qa - kj)
    band = dist <= rad
    ok_prev = band & (kj >= jnp.where(it > 0, -rad, 0))
    ok_next = band & (kj < jnp.where(it < nt - 1, 2 * rad, rad))
    ok_both = ok_prev & ok_next

    for d in DILATIONS:
        nsub = ATTN_TILE // (rad * d)
        dist_f = (dist * d).astype(F32)
        biases = []
        for hh in range(LANES // ATTN_HEAD_DIM):
            base = (-slopes_ref[pair * (LANES // ATTN_HEAD_DIM) + hh] * dist_f) * LOG2_E
            biases.append({(False, False): jnp.where(band, base, NEG_INF),
                           (True, False): jnp.where(ok_prev, base, NEG_INF),
                           (False, True): jnp.where(ok_next, base, NEG_INF),
                           (True, True): jnp.where(ok_both, base, NEG_INF)})
        specs = [(r, i) for r in range(d) for i in range(nsub)]
        for b0 in range(0, len(specs), ATTN_BATCH):
            _attn_blocks(accessors(d), nsub, specs[b0:b0 + ATTN_BATCH], biases)
    for r4 in range(sub):
        rows = pl.ds(r4, slab, stride=sub)
        m1, m2 = m_s[rows, :], m2_s[r4]
        m_tot = jnp.maximum(m1, m2)
        w1, w2 = jnp.exp2(m1 - m_tot), jnp.exp2(m2 - m_tot)
        acc_s[rows, :] = (w1 * acc_s[rows, :] + w2 * acc2_s[r4]) / (w1 * l_s[rows, :] + w2 * l2_s[r4])
    o_ref[...] = acc_s[...].astype(BF16)


def _attn_blocks(accessors, nsub, specs, biases):
    rad = ATTN_RADIUS
    lo = lax.broadcasted_iota(jnp.int32, (1, LANES), 1) < ATTN_HEAD_DIM
    q_get, kv_get, target, first = accessors

    blocks = []
    for r, i in specs:
        kvs = [kv_get(r, i, piece) for piece in (-1, 0, 1)]
        blocks.append(dict(target=target(r, i), edge=(i == 0, i == nsub - 1), q=q_get(r, i),
                           kw=jnp.concatenate([k for k, _ in kvs], axis=0).astype(BF16),
                           vw=jnp.concatenate([v for _, v in kvs], axis=0).astype(BF16)))
    for blk in blocks:
        q2 = jnp.concatenate([jnp.where(lo, blk["q"], 0.0), jnp.where(lo, 0.0, blk["q"])], axis=0)
        blk["s"] = _dot_nt(q2, blk["kw"]) + jnp.concatenate([b[blk["edge"]] for b in biases], axis=0)
    for blk in blocks:
        blk["m"] = jnp.max(blk["s"], axis=-1, keepdims=True)
    for blk in blocks:
        blk["p"] = jnp.exp2(blk["s"] - blk["m"])
    for blk in blocks:
        blk["l"] = jnp.sum(blk["p"], axis=-1, keepdims=True)
        blk["acc"] = _dot(blk["p"], blk["vw"])
    for blk in blocks:
        (m_s, l_s, acc_s), idx = blk["target"]
        idx = idx + (slice(None),)
        m_new = jnp.where(lo, blk["m"][:rad], blk["m"][rad:])
        l_new = jnp.where(lo, blk["l"][:rad], blk["l"][rad:])
        acc_new = jnp.where(lo, blk["acc"][:rad], blk["acc"][rad:])
        if first:
            m_s[idx], l_s[idx], acc_s[idx] = m_new, l_new, acc_new
        else:
            m_old = m_s[idx]
            m_tot = jnp.maximum(m_old, m_new)
            w_old, w_new = jnp.exp2(m_old - m_tot), jnp.exp2(m_new - m_tot)
            m_s[idx] = m_tot
            l_s[idx] = w_old * l_s[idx] + w_new * l_new
            acc_s[idx] = w_old * acc_s[idx] + w_new * acc_new


def _dilated_attn(q, k, v, slopes, batch, seq):
    n, d_attn = q.shape
    assert seq % ATTN_TILE == 0
    nt = seq // ATTN_TILE
    last = n // ATTN_TILE - 1
    slab = ATTN_TILE // ATTN_SPLIT
    cur = pl.BlockSpec((ATTN_TILE, LANES), lambda b, i, p: (b * nt + i, p))
    prev = pl.BlockSpec((ATTN_TILE, LANES), lambda b, i, p: (jnp.maximum(b * nt + i - 1, 0), p))
    nxt = pl.BlockSpec((ATTN_TILE, LANES), lambda b, i, p: (jnp.minimum(b * nt + i + 1, last), p))
    return pl.pallas_call(
        functools.partial(_dilated_attn_body, nt=nt),
        grid=(batch, nt, d_attn // LANES),
        in_specs=[pl.BlockSpec(memory_space=pltpu.SMEM), cur, prev, cur, nxt, prev, cur, nxt],
        out_specs=cur,
        out_shape=jax.ShapeDtypeStruct((n, d_attn), BF16),
        scratch_shapes=[pltpu.VMEM((ATTN_TILE, LANES), F32)] * 3
        + [pltpu.VMEM((ATTN_SPLIT, slab, LANES), F32), pltpu.VMEM((3 * ATTN_SPLIT, slab, LANES), F32),
           pltpu.VMEM((3 * ATTN_SPLIT, slab, LANES), F32)] + [pltpu.VMEM((ATTN_SPLIT, slab, LANES), F32)] * 3,
        compiler_params=_params("parallel", "parallel", "arbitrary"),
    )(slopes, q, k, k, k, v, v, v)


def _dilated_attention(x, batch, seq, g, w_in, w_out, q_norm, k_norm):
    q, k, v = _attn_proj(x, g, w_in, q_norm, k_norm)
    heads = q.shape[1] // ATTN_HEAD_DIM
    slopes = 2.0 ** (-8.0 * jnp.arange(1, heads + 1, dtype=F32) / heads)
    return [(_dilated_attn(q, k, v, slopes, batch, seq), w_out)]


def _sgu_rwkv_mixer(x, batch, seq, g, w_in, w_out, sgu_norm, sgu_w, sgu_b, mu_rkv, mu_wa, w0, w_up,
                    a0, a_up, k_k, k_a, r_k, gn_w, gn_b):
    d_a = sgu_norm.shape[0]
    d_b = k_k.shape[0]
    ya, rkv, gate, wa = _proj_sgu(x, g, w_in, sgu_norm, sgu_w, sgu_b, d_a=d_a, d_b=d_b)
    yf, bonf, yb, bonb = _rwkv_scan(rkv, wa, batch, seq, mu_rkv, mu_wa, w0, w_up, a0, a_up,
                                    k_k, k_a, r_k.reshape(-1))
    y_rwkv = _rwkv_post(yf, yb, bonf, bonb, gate, gn_w, gn_b)
    return [(ya, w_out[:d_a]), (y_rwkv, w_out[d_a:])]


def kernel(x_prompt, x_sample, ffn1_norm, ffn1_w_in, ffn1_w_out, mix_norm, ffn2_norm, ffn2_w_in,
           ffn2_w_out, block_norm, ab_w_in, ab_w_out, sgu_norm, sgu_w, sgu_b, rwkv_mu_rkv, rwkv_mu_wa,
           rwkv_w0, rwkv_w_up, rwkv_a0, rwkv_a_up, rwkv_k_k, rwkv_k_a, rwkv_r_k, rwkv_gn_w, rwkv_gn_b,
           attn_w_in, attn_w_out, attn_q_norm, attn_k_norm):
    depth = ffn1_norm.shape[0]

    def trunk(x3):
        batch, seq, d = x3.shape
        x = x3.reshape(batch * seq, d)
        for i in range(depth):
            j = i // 2
            x = _ffn(x, ffn1_norm[i], ffn1_w_in[i], ffn1_w_out[i])
            if i % 2 == 0:
                mix = _sgu_rwkv_mixer(x, batch, seq, mix_norm[i], ab_w_in[j], ab_w_out[j], sgu_norm[j],
                                      sgu_w[j], sgu_b[j], rwkv_mu_rkv[j], rwkv_mu_wa[j], rwkv_w0[j],
                                      rwkv_w_up[j], rwkv_a0[j], rwkv_a_up[j], rwkv_k_k[j], rwkv_k_a[j],
                                      rwkv_r_k[j], rwkv_gn_w[j], rwkv_gn_b[j])
            else:
                mix = _dilated_attention(x, batch, seq, mix_norm[i], attn_w_in[j], attn_w_out[j],
                                         attn_q_norm[j], attn_k_norm[j])
            x = _ffn(x, ffn2_norm[i], ffn2_w_in[i], ffn2_w_out[i], g2=block_norm[i], pre=mix)
        return x.reshape(batch, seq, d)

    return trunk(x_prompt), trunk(x_sample)
```
